```python
import numpy as np
import jax
import jax.numpy as jnp
from jax import lax

D_MODEL = 2048
BATCH = 8
SEQ = 2048
DEPTH = 1
DEC_BATCH = 2
DEC_SEQ = 8192
PAST_LEN = 128

HEAD_DIM = 128
N_HEADS_A = (D_MODEL // HEAD_DIM) // 2
N_HEADS_B = (D_MODEL // HEAD_DIM) - N_HEADS_A
N_KV_B = max(1, N_HEADS_B // 4)
WIDTH_A = N_HEADS_A * HEAD_DIM
WIDTH_B = N_HEADS_B * HEAD_DIM
KV_WIDTH_B = N_KV_B * HEAD_DIM
IN_WIDTH = 3 * WIDTH_A + WIDTH_B + 2 * KV_WIDTH_B
DILATED = ((128, 1), (512, 4), (2048, 16))
Q_BLOCK = 128
GRID_W = 64
ROPE_THETA = 10000.0
NUM_BUCKETS = 32
MAX_DISTANCE = 1024
N_GROUPS = 4
EXPERTS_PER_GROUP = 8
N_EXPERTS = N_GROUPS * EXPERTS_PER_GROUP
TOP_K = 2
D_EXPERT = D_MODEL // 2
MOE_BLOCK = 128
EPS = 1e-6
NEG_INF = -1e30

kernel_name = 'hymba_dilated_gqa_hmoe_encoder'


def rms_norm(x, g):
    xf = x.astype(jnp.float32)
    y = xf * lax.rsqrt(jnp.mean(xf * xf, axis=-1, keepdims=True) + EPS)
    return (y * g.astype(jnp.float32)).astype(x.dtype)


def t5_buckets(rel):
    nb = NUM_BUCKETS // 2
    max_exact = nb // 2
    n = np.abs(rel)
    large = max_exact + (np.log(np.maximum(n, 1) / max_exact) / np.log(MAX_DISTANCE / max_exact)
                         * (nb - max_exact)).astype(np.int32)
    large = np.minimum(large, nb - 1)
    return (rel > 0).astype(np.int32) * nb + np.where(n < max_exact, n, large).astype(np.int32)


def dilated_attention(q, k, v, rel_bias):
    b, s, h, dh = q.shape
    pad = max(w // 2 for w, _ in DILATED)
    kp = jnp.pad(k, ((0, 0), (pad, pad), (0, 0), (0, 0)))
    vp = jnp.pad(v, ((0, 0), (pad, pad), (0, 0), (0, 0)))
    scale = HEAD_DIM ** -0.5
    branches = []
    for w, d in DILATED:
        r = w // (2 * d)
        offs = d * np.arange(-r, r + 1, dtype=np.int32)
        bias = rel_bias[t5_buckets(offs)].T.astype(jnp.float32)
        branches.append((jnp.asarray(offs), bias))

    def block(s0):
        qb = lax.dynamic_slice_in_dim(q, s0, Q_BLOCK, axis=1)
        pos = s0 + jnp.arange(Q_BLOCK, dtype=jnp.int32)
        outs, lses = [], []
        for offs, bias in branches:
            kpos = pos[:, None] + offs[None, :]
            valid = (kpos >= 0) & (kpos < s)
            kg = jnp.take(kp, kpos + pad, axis=1)
            vg = jnp.take(vp, kpos + pad, axis=1)
            logits = jnp.einsum('bqhd,bqnhd->bhqn', qb, kg).astype(jnp.float32) * scale
            logits = logits + bias[None, :, None, :]
            logits = jnp.where(valid[None, None], logits, NEG_INF)
            m = jnp.max(logits, axis=-1, keepdims=True)
            p = jnp.exp(logits - m)
            l = jnp.sum(p, axis=-1, keepdims=True)
            outs.append(jnp.einsum('bhqn,bqnhd->bqhd', (p / l).astype(v.dtype), vg).astype(jnp.float32))
            lses.append((m + jnp.log(l))[..., 0])
        wts = jax.nn.softmax(jnp.stack(lses), axis=0)
        wts = jnp.transpose(wts, (0, 1, 3, 2))[..., None]
        out = sum(wts[i] * outs[i] for i in range(len(outs)))
        return out.astype(q.dtype)

    starts = jnp.arange(s // Q_BLOCK, dtype=jnp.int32) * Q_BLOCK
    y = lax.map(block, starts)
    return jnp.moveaxis(y, 0, 1).reshape(b, s, h * dh)


def axial_angles(s):
    rows = s // GRID_W
    r, c = jnp.meshgrid(jnp.arange(rows, dtype=jnp.float32), jnp.arange(GRID_W, dtype=jnp.float32), indexing='ij')
    n_freq = HEAD_DIM // 4
    inv = ROPE_THETA ** (-jnp.arange(n_freq, dtype=jnp.float32) / n_freq)
    return r.reshape(-1)[:, None] * inv, c.reshape(-1)[:, None] * inv


def axial_rope(x, ang_row, ang_col):
    xf = x.astype(jnp.float32)
    half = x.shape[-1] // 2

    def rot(y, ang):
        c = jnp.cos(ang)[:, None, :]
        sn = jnp.sin(ang)[:, None, :]
        y1, y2 = jnp.split(y, 2, axis=-1)
        return jnp.concatenate([y1 * c - y2 * sn, y1 * sn + y2 * c], axis=-1)

    return jnp.concatenate([rot(xf[..., :half], ang_row), rot(xf[..., half:], ang_col)], axis=-1).astype(x.dtype)


def gqa_attention(q, k, v):
    b, s, h, dh = q.shape
    rep = N_HEADS_B // N_KV_B
    qg = q.reshape(b, s, N_KV_B, rep, dh)
    scale = HEAD_DIM ** -0.5

    def block(s0):
        qb = lax.dynamic_slice_in_dim(qg, s0, Q_BLOCK, axis=1)
        logits = jnp.einsum('bqgrd,bkgd->bgrqk', qb, k).astype(jnp.float32) * scale
        p = jax.nn.softmax(logits, axis=-1)
        return jnp.einsum('bgrqk,bkgd->bqgrd', p.astype(v.dtype), v)

    starts = jnp.arange(s // Q_BLOCK, dtype=jnp.int32) * Q_BLOCK
    y = lax.map(block, starts)
    return jnp.moveaxis(y, 0, 1).reshape(b, s, h * dh)


def hierarchical_moe(h, rg_w, rg_b, re_w, re_b, w_gate, w_up, w_down):
    b, s, d = h.shape
    t = b * s
    ht = h.reshape(t, d)
    g_logits = (ht @ rg_w + rg_b).astype(jnp.float32)
    g_prob = jax.nn.softmax(g_logits, axis=-1)
    g_sel = jnp.argmax(g_logits, axis=-1)
    g_w = jnp.take_along_axis(g_prob, g_sel[:, None], axis=1)[:, 0]
    e_logits = (ht @ re_w + re_b).astype(jnp.float32).reshape(t, N_GROUPS, EXPERTS_PER_GROUP)
    e_logits = jnp.take_along_axis(e_logits, g_sel[:, None, None], axis=1)[:, 0]
    top_l, top_i = lax.top_k(e_logits, TOP_K)
    gate = g_w[:, None] * jax.nn.softmax(top_l, axis=-1)
    expert = g_sel[:, None].astype(jnp.int32) * EXPERTS_PER_GROUP + top_i.astype(jnp.int32)

    a = t * TOP_K
    e_flat = expert.reshape(a)
    tok_flat = jnp.repeat(jnp.arange(t, dtype=jnp.int32), TOP_K)
    w_flat = gate.reshape(a)
    order = jnp.argsort(e_flat)
    se = e_flat[order]
    counts = jnp.bincount(e_flat, length=N_EXPERTS).astype(jnp.int32)
    starts = jnp.cumsum(counts) - counts
    pcounts = (counts + MOE_BLOCK - 1) // MOE_BLOCK * MOE_BLOCK
    pends = jnp.cumsum(pcounts)
    pstarts = pends - pcounts
    dest = pstarts[se] + (jnp.arange(a, dtype=jnp.int32) - starts[se])
    p_rows = a + N_EXPERTS * MOE_BLOCK
    nblk = p_rows // MOE_BLOCK
    slot_tok = jnp.zeros((p_rows,), jnp.int32).at[dest].set(tok_flat[order])
    slot_w = jnp.zeros((p_rows,), jnp.float32).at[dest].set(w_flat[order])
    blk_exp = jnp.minimum(jnp.searchsorted(pends, jnp.arange(nblk, dtype=jnp.int32) * MOE_BLOCK, side='right'),
                          N_EXPERTS - 1).astype(jnp.int32)
    xs = ht[slot_tok].reshape(nblk, MOE_BLOCK, d)

    def expert_block(args):
        xb, e = args
        hid = jax.nn.silu(xb @ w_gate[e]) * (xb @ w_up[e])
        return hid @ w_down[e]

    ys = lax.map(expert_block, (xs, blk_exp)).reshape(p_rows, d)
    out = jnp.zeros((t, d), jnp.float32).at[slot_tok].add(ys.astype(jnp.float32) * slot_w[:, None])
    return out.astype(h.dtype).reshape(b, s, d)


def encoder_layer(x, norm1_g, w_in, qn_a, kn_a, qn_b, kn_b, rel_bias, on_a, on_b, w_out,
                  norm2_g, rg_w, rg_b, re_w, re_b, w_gate, w_up, w_down):
    b, s, d = x.shape
    h = rms_norm(x, norm1_g)
    proj = h @ w_in
    cuts = np.cumsum([WIDTH_A, WIDTH_A, WIDTH_A, WIDTH_B, KV_WIDTH_B]).tolist()
    qa, ka, va, qb, kb, vb = jnp.split(proj, cuts, axis=-1)
    qa = rms_norm(qa.reshape(b, s, N_HEADS_A, HEAD_DIM), qn_a)
    ka = rms_norm(ka.reshape(b, s, N_HEADS_A, HEAD_DIM), kn_a)
    va = va.reshape(b, s, N_HEADS_A, HEAD_DIM)
    oa = dilated_attention(qa, ka, va, rel_bias)
    ang_r, ang_c = axial_angles(s)
    qb = axial_rope(rms_norm(qb.reshape(b, s, N_HEADS_B, HEAD_DIM), qn_b), ang_r, ang_c)
    kb = axial_rope(rms_norm(kb.reshape(b, s, N_KV_B, HEAD_DIM), kn_b), ang_r, ang_c)
    vb = vb.reshape(b, s, N_KV_B, HEAD_DIM)
    ob = gqa_attention(qb, kb, vb)
    mix = jnp.concatenate([rms_norm(oa, on_a), rms_norm(ob, on_b)], axis=-1)
    x = x + mix @ w_out
    x = x + hierarchical_moe(rms_norm(x, norm2_g), rg_w, rg_b, re_w, re_b, w_gate, w_up, w_down)
    return x


def setup_inputs(seed: int = 0) -> dict:
    key = jax.random.key(seed)
    ks = jax.random.split(key, 24)
    f32 = jnp.float32
    nrm = lambda k, shp: jax.random.normal(k, shp, f32)
    return {
        'x_prompt': nrm(ks[0], (BATCH, SEQ, D_MODEL)),
        'x_sample': nrm(ks[1], (DEC_BATCH, DEC_SEQ, D_MODEL)),
        'norm1_g': 1.0 + 0.02 * nrm(ks[2], (DEPTH, D_MODEL)),
        'w_in': nrm(ks[3], (DEPTH, D_MODEL, IN_WIDTH)) * D_MODEL ** -0.5,
        'qn_a': 1.0 + 0.02 * nrm(ks[4], (DEPTH, HEAD_DIM)),
        'kn_a': 1.0 + 0.02 * nrm(ks[5], (DEPTH, HEAD_DIM)),
        'qn_b': 1.0 + 0.02 * nrm(ks[6], (DEPTH, HEAD_DIM)),
        'kn_b': 1.0 + 0.02 * nrm(ks[7], (DEPTH, HEAD_DIM)),
        'rel_bias': 0.3 * nrm(ks[8], (NUM_BUCKETS, N_HEADS_A)),
        'on_a': 1.0 + 0.02 * nrm(ks[9], (DEPTH, WIDTH_A)),
        'on_b': 1.0 + 0.02 * nrm(ks[10], (DEPTH, WIDTH_B)),
        'w_out': nrm(ks[11], (DEPTH, WIDTH_A + WIDTH_B, D_MODEL)) * (WIDTH_A + WIDTH_B) ** -0.5,
        'norm2_g': 1.0 + 0.02 * nrm(ks[12], (DEPTH, D_MODEL)),
        'rg_w': nrm(ks[13], (DEPTH, D_MODEL, N_GROUPS)) * D_MODEL ** -0.5,
        'rg_b': 0.01 * nrm(ks[14], (DEPTH, N_GROUPS)),
        're_w': nrm(ks[15], (DEPTH, D_MODEL, N_EXPERTS)) * D_MODEL ** -0.5,
        're_b': 0.01 * nrm(ks[16], (DEPTH, N_EXPERTS)),
        'w_gate': nrm(ks[17], (DEPTH, N_EXPERTS, D_MODEL, D_EXPERT)) * D_MODEL ** -0.5,
        'w_up': nrm(ks[18], (DEPTH, N_EXPERTS, D_MODEL, D_EXPERT)) * D_MODEL ** -0.5,
        'w_down': nrm(ks[19], (DEPTH, N_EXPERTS, D_EXPERT, D_MODEL)) * D_EXPERT ** -0.5,
    }


def reference(x_prompt, x_sample, norm1_g, w_in, qn_a, kn_a, qn_b, kn_b, rel_bias, on_a, on_b, w_out,
              norm2_g, rg_w, rg_b, re_w, re_b, w_gate, w_up, w_down):
    def trunk(x):
        for l in range(DEPTH):
            x = encoder_layer(x, norm1_g[l], w_in[l], qn_a[l], kn_a[l], qn_b[l], kn_b[l], rel_bias,
                              on_a[l], on_b[l], w_out[l], norm2_g[l], rg_w[l], rg_b[l], re_w[l], re_b[l],
                              w_gate[l], w_up[l], w_down[l])
        return x

    y_prompt = trunk(x_prompt)
    y_sample = trunk(x_sample)
    return (y_prompt, y_sample)
```

```python
import functools

import numpy as np
import jax
import jax.numpy as jnp
from jax import lax
from jax.experimental import pallas as pl
from jax.experimental.pallas import tpu as pltpu

D_MODEL = 2048
HEAD_DIM = 128
N_HEADS_A = 8
N_HEADS_B = 8
N_KV_B = 2
WIDTH_A = N_HEADS_A * HEAD_DIM
WIDTH_B = N_HEADS_B * HEAD_DIM
KV_WIDTH_B = N_KV_B * HEAD_DIM
IN_WIDTH = 3 * WIDTH_A + WIDTH_B + 2 * KV_WIDTH_B
DILATED = ((128, 1), (512, 4), (2048, 16))
GRID_W = 64
ROPE_THETA = 10000.0
NUM_BUCKETS = 32
MAX_DISTANCE = 1024
N_GROUPS = 4
EXPERTS_PER_GROUP = 8
N_EXPERTS = N_GROUPS * EXPERTS_PER_GROUP
TOP_K = 2
D_EXPERT = D_MODEL // 2
EPS = 1e-6
NEG_INF = -1e30

LANES = 128
SEC = 4 * HEAD_DIM
N_SEC = IN_WIDTH // SEC
SEC_QA, SEC_KA, SEC_VA, SEC_QB, SEC_KVB = 0, 2, 4, 6, 8
BAND = 64
VMEM_LIMIT = 56 * 1024 * 1024

TM_IN = 512
TQ_DIL = 128
TQ_GQA = 128
TK_GQA = 512
TM_OUT = 256
TM_ROUTE = 256
TT_ROWS = 256
MOE_ROWS = 256

F32 = jnp.float32
BF16 = jnp.bfloat16


def _cparams(n_axes):
    return pltpu.CompilerParams(dimension_semantics=("arbitrary",) * n_axes, vmem_limit_bytes=VMEM_LIMIT)


def _in_proj_kernel(x_ref, g1_ref, w_ref, gain_ref, cos_ref, sin_ref, o_ref, h_ref):
    j = pl.program_id(1)

    @pl.when(j == 0)
    def _():
        x = x_ref[...]
        ms = jnp.mean(x * x, axis=-1, keepdims=True)
        h_ref[...] = (x * lax.rsqrt(ms + EPS) * g1_ref[...]).astype(BF16)

    p = jnp.dot(h_ref[...], w_ref[...], preferred_element_type=F32)
    lane = lax.broadcasted_iota(jnp.int32, (p.shape[0], HEAD_DIM), 1)
    first_quarter = (lane & 32) == 0

    def emit(norm, rope):
        for hh in range(SEC // HEAD_DIM):
            sl = slice(hh * HEAD_DIM, (hh + 1) * HEAD_DIM)
            y = p[:, sl]
            if norm[hh]:
                ms = jnp.mean(y * y, axis=-1, keepdims=True)
                y = y * lax.rsqrt(ms + EPS)
            y = y * gain_ref[:, sl]
            if rope[hh]:
                partner = jnp.where(first_quarter, pltpu.roll(y, HEAD_DIM - 32, 1), pltpu.roll(y, 32, 1))
                y = y * cos_ref[...] + partner * sin_ref[...]
            o_ref[:, sl] = y.astype(BF16)

    yes, no = (True,) * 4, (False,) * 4

    @pl.when(j < SEC_VA)
    def _():
        emit(yes, no)

    @pl.when((j >= SEC_VA) & (j < SEC_QB))
    def _():
        emit(no, no)

    @pl.when((j >= SEC_QB) & (j < SEC_KVB))
    def _():
        emit(yes, yes)

    @pl.when(j == SEC_KVB)
    def _():
        emit((True, True, False, False), (True, True, False, False))


def _in_proj(x, g1, w_in, gain, cos, sin, s):
    t = x.shape[0]
    tm = TM_IN
    spt = s // tm
    return pl.pallas_call(
        _in_proj_kernel,
        grid=(t // tm, N_SEC),
        in_specs=[
            pl.BlockSpec((tm, D_MODEL), lambda i, j: (i, 0)),
            pl.BlockSpec((1, D_MODEL), lambda i, j: (0, 0)),
            pl.BlockSpec((D_MODEL, SEC), lambda i, j: (0, j)),
            pl.BlockSpec((1, SEC), lambda i, j: (0, j)),
            pl.BlockSpec((tm, HEAD_DIM), lambda i, j: (i % spt, 0)),
            pl.BlockSpec((tm, HEAD_DIM), lambda i, j: (i % spt, 0)),
        ],
        out_specs=pl.BlockSpec((None, tm, SEC), lambda i, j: (j, i, 0)),
        out_shape=jax.ShapeDtypeStruct((N_SEC, t, SEC), BF16),
        scratch_shapes=[pltpu.VMEM((tm, D_MODEL), BF16)],
        compiler_params=_cparams(2),
        name="in_proj",
    )(x, g1, w_in, gain, cos, sin)


def _dilated_kernel(q_ref, kp_ref, kc_ref, kn_ref, vp_ref, vc_ref, vn_ref, a_ref, o_ref, lse_ref):
    tq = q_ref.shape[0]
    lane = lax.broadcasted_iota(jnp.int32, (tq, LANES), 1)
    lse_all = jnp.zeros((tq, LANES), F32)
    nt = (((1,), (1,)), ((), ()))
    for hh in range(SEC // HEAD_DIM):
        sl = slice(hh * HEAD_DIM, (hh + 1) * HEAD_DIM)
        q = q_ref[:, sl]
        s = jnp.concatenate(
            [lax.dot_general(q, k_ref[:, sl], nt, preferred_element_type=F32) for k_ref in (kp_ref, kc_ref, kn_ref)],
            axis=1) + a_ref[hh]
        m = jnp.max(s, axis=-1, keepdims=True)
        p = jnp.exp(s - m)
        l = jnp.sum(p, axis=-1, keepdims=True)
        pb = p.astype(BF16)
        o = (jnp.dot(pb[:, :tq], vp_ref[:, sl], preferred_element_type=F32)
             + jnp.dot(pb[:, tq:2 * tq], vc_ref[:, sl], preferred_element_type=F32)
             + jnp.dot(pb[:, 2 * tq:], vn_ref[:, sl], preferred_element_type=F32))
        o_ref[:, sl] = o / l
        lse_all = jnp.where(lane == hh, m + jnp.log(l), lse_all)
    lse_ref[...] = lse_all


def _dilated_branch(qkv, a_tab, b, s, d):
    ln = s // d
    tq = TQ_DIL
    nq = ln // tq
    view = qkv.reshape(N_SEC, b, ln, d * SEC)

    def variant(qi):
        if nq == 1:
            return 3
        return jnp.where(qi == 0, 0, jnp.where(qi == nq - 1, 2, 1))

    def kv_spec(sec, shift):
        return pl.BlockSpec((None, None, tq, SEC),
                            lambda bi, r, qi, hg: (sec + hg, bi, jnp.clip(qi + shift, 0, nq - 1), r))

    o, lse = pl.pallas_call(
        _dilated_kernel,
        grid=(b, d, nq, 2),
        in_specs=[pl.BlockSpec((None, None, tq, SEC), lambda bi, r, qi, hg: (SEC_QA + hg, bi, qi, r)),
                  kv_spec(SEC_KA, -1), kv_spec(SEC_KA, 0), kv_spec(SEC_KA, 1),
                  kv_spec(SEC_VA, -1), kv_spec(SEC_VA, 0), kv_spec(SEC_VA, 1),
                  pl.BlockSpec((None, None, 4, tq, 3 * tq), lambda bi, r, qi, hg: (variant(qi), hg, 0, 0, 0))],
        out_specs=[pl.BlockSpec((None, tq, SEC), lambda bi, r, qi, hg: (bi, qi, 2 * r + hg)),
                   pl.BlockSpec((None, None, tq, LANES), lambda bi, r, qi, hg: (hg, bi, qi, r))],
        out_shape=[jax.ShapeDtypeStruct((b, ln, d * WIDTH_A), F32),
                   jax.ShapeDtypeStruct((2, b, ln, d * LANES), F32)],
        compiler_params=_cparams(4),
        name=f"dilated_d{d}",
    )(view, view, view, view, view, view, view, a_tab)
    return o.reshape(b * s, WIDTH_A), lse.reshape(2, b * s, LANES)


def _t5_buckets(rel):
    nb = NUM_BUCKETS // 2
    max_exact = nb // 2
    n = np.abs(rel)
    large = max_exact + (np.log(np.maximum(n, 1) / max_exact) / np.log(MAX_DISTANCE / max_exact)
                         * (nb - max_exact)).astype(np.int32)
    large = np.minimum(large, nb - 1)
    return (rel > 0).astype(np.int32) * nb + np.where(n < max_exact, n, large).astype(np.int32)


def _band_tables(rel_bias, d):
    tq = TQ_DIL
    r = np.arange(tq)[:, None]
    c = np.arange(3 * tq)[None, :]
    rel = (c - tq) - r
    in_band = np.abs(rel) <= BAND
    bucket = _t5_buckets(d * np.clip(rel, -BAND, BAND))
    bias = jnp.transpose(rel_bias.astype(F32)[bucket], (2, 0, 1))
    prev_ok, next_ok = c >= tq, c < 2 * tq
    tabs = []
    for keep in (in_band & prev_ok, in_band, in_band & next_ok, in_band & prev_ok & next_ok):
        tabs.append(jnp.where(keep[None], bias, NEG_INF))
    return jnp.stack(tabs).reshape(4, 2, 4, tq, 3 * tq)


def _gqa_kernel(q_ref, k_ref, v_ref, o_ref, *, tk):
    tq = q_ref.shape[0]
    nk = k_ref.shape[0] // tk
    rep = SEC // HEAD_DIM
    q = jnp.concatenate([q_ref[:, hh * HEAD_DIM:(hh + 1) * HEAD_DIM] for hh in range(rep)], axis=0)
    nt = (((1,), (1,)), ((), ()))

    def body(c, carry):
        m, l, acc = carry
        start = pl.multiple_of(c * tk, tk)
        s = lax.dot_general(q, k_ref[pl.ds(start, tk), :], nt, preferred_element_type=F32)
        m_new = jnp.maximum(m, jnp.max(s, axis=-1, keepdims=True))
        alpha = jnp.exp(m - m_new)
        p = jnp.exp(s - m_new)
        l = alpha * l + jnp.sum(p, axis=-1, keepdims=True)
        acc = alpha * acc + jnp.dot(p.astype(BF16), v_ref[pl.ds(start, tk), :], preferred_element_type=F32)
        return m_new, l, acc

    init = (jnp.full((rep * tq, 1), NEG_INF, F32), jnp.zeros((rep * tq, 1), F32),
            jnp.zeros((rep * tq, HEAD_DIM), F32))
    _, l, acc = lax.fori_loop(0, nk, body, init)
    o = acc / l
    for hh in range(rep):
        o_ref[:, hh * HEAD_DIM:(hh + 1) * HEAD_DIM] = o[hh * tq:(hh + 1) * tq]


def _gqa(qkv, b, s):
    tq = TQ_GQA
    view = qkv.reshape(N_SEC, b, s, SEC)
    o = pl.pallas_call(
        functools.partial(_gqa_kernel, tk=min(TK_GQA, s)),
        grid=(b, N_KV_B, s // tq),
        in_specs=[pl.BlockSpec((None, None, tq, SEC), lambda bi, g, qi: (SEC_QB + g, bi, qi, 0)),
                  pl.BlockSpec((None, None, s, HEAD_DIM), lambda bi, g, qi: (SEC_KVB, bi, 0, g)),
                  pl.BlockSpec((None, None, s, HEAD_DIM), lambda bi, g, qi: (SEC_KVB, bi, 0, N_KV_B + g))],
        out_specs=pl.BlockSpec((None, tq, SEC), lambda bi, g, qi: (bi, qi, g)),
        out_shape=jax.ShapeDtypeStruct((b, s, WIDTH_B), F32),
        compiler_params=_cparams(3),
        name="gqa",
    )(view, view, view)
    return o.reshape(b * s, WIDTH_B)


def _out_proj_kernel(o1_ref, o2_ref, o3_ref, l1_ref, l2_ref, l3_ref, ob_ref, x_ref, ga_ref, gb_ref, w_ref, y_ref):
    lses = (l1_ref[...], l2_ref[...], l3_ref[...])
    m = jnp.maximum(jnp.maximum(lses[0], lses[1]), lses[2])
    es = [jnp.exp(v - m) for v in lses]
    tot = es[0] + es[1] + es[2]
    ws = [e / tot for e in es]
    heads = []
    for h in range(N_HEADS_A):
        sl = slice(h * HEAD_DIM, (h + 1) * HEAD_DIM)
        hg, hl = divmod(h, 4)
        acc = None
        for w, o_ref in zip(ws, (o1_ref, o2_ref, o3_ref)):
            term = w[hg][:, hl:hl + 1] * o_ref[:, sl]
            acc = term if acc is None else acc + term
        heads.append(acc)
    oa = jnp.concatenate(heads, axis=1)
    oa = oa * lax.rsqrt(jnp.mean(oa * oa, axis=-1, keepdims=True) + EPS) * ga_ref[...]
    ob = ob_ref[...]
    ob = ob * lax.rsqrt(jnp.mean(ob * ob, axis=-1, keepdims=True) + EPS) * gb_ref[...]
    y = (jnp.dot(oa.astype(BF16), w_ref[:WIDTH_A, :], preferred_element_type=F32)
         + jnp.dot(ob.astype(BF16), w_ref[WIDTH_A:, :], preferred_element_type=F32))
    y_ref[...] = x_ref[...] + y


def _out_proj(os_, lses, ob, x, on_a, on_b, w_out):
    t = x.shape[0]
    tm = TM_OUT
    row = lambda width: pl.BlockSpec((tm, width), lambda i: (i, 0))
    lse_spec = pl.BlockSpec((2, tm, LANES), lambda i: (0, i, 0))
    const = lambda shape: pl.BlockSpec(shape, lambda i: (0, 0))
    return pl.pallas_call(
        _out_proj_kernel,
        grid=(t // tm,),
        in_specs=[row(WIDTH_A), row(WIDTH_A), row(WIDTH_A), lse_spec, lse_spec, lse_spec, row(WIDTH_B),
                  row(D_MODEL), const((1, WIDTH_A)), const((1, WIDTH_B)), const((WIDTH_A + WIDTH_B, D_MODEL))],
        out_specs=row(D_MODEL),
        out_shape=jax.ShapeDtypeStruct((t, D_MODEL), F32),
        compiler_params=_cparams(1),
        name="out_proj",
    )(*os_, *lses, ob, x, on_a, on_b, w_out)


def _router_kernel(x_ref, g2_ref, w_ref, b_ref, h_ref, meta_ref, cnt_ref, run_ref):
    tm = x_ref.shape[0]

    @pl.when(pl.program_id(0) == 0)
    def _():
        run_ref[...] = jnp.zeros_like(run_ref)

    x = x_ref[...]
    h = x * lax.rsqrt(jnp.mean(x * x, axis=-1, keepdims=True) + EPS) * g2_ref[...]
    h_ref[...] = h
    logits = jnp.dot(h.astype(BF16), w_ref[...], preferred_element_type=F32) + b_ref[...]
    gl, el = logits[:, :LANES], logits[:, LANES:]
    lane = lax.broadcasted_iota(jnp.int32, (tm, LANES), 1).astype(F32)
    ninf = jnp.float32(-jnp.inf)
    big = jnp.float32(LANES)

    def first_argmax(v):
        top = jnp.max(v, axis=-1, keepdims=True)
        return top, jnp.min(jnp.where(v == top, lane, big), axis=-1, keepdims=True)

    gmask = lane < N_GROUPS
    gtop, gsel = first_argmax(jnp.where(gmask, gl, ninf))
    g_w = 1.0 / jnp.sum(jnp.where(gmask, jnp.exp(gl - gtop), 0.0), axis=-1, keepdims=True)
    lo = gsel * EXPERTS_PER_GROUP
    elm = jnp.where((lane >= lo) & (lane < lo + EXPERTS_PER_GROUP), el, ninf)
    t1, i1 = first_argmax(elm)
    t2, i2 = first_argmax(jnp.where(lane == i1, ninf, elm))
    e2 = jnp.exp(t2 - t1)
    gate1 = g_w * (1.0 / (1.0 + e2))
    gate2 = g_w * (e2 / (1.0 + e2))

    hot1, hot2 = lane == i1, lane == i2
    hot = (hot1 | hot2).astype(F32)
    r_i = lax.broadcasted_iota(jnp.int32, (tm, tm), 0)
    c_i = lax.broadcasted_iota(jnp.int32, (tm, tm), 1)
    before = (c_i < r_i).astype(BF16)
    rank = jnp.dot(before, hot.astype(BF16), preferred_element_type=F32) + run_ref[...]
    rank1 = jnp.sum(jnp.where(hot1, rank, 0.0), axis=-1, keepdims=True)
    rank2 = jnp.sum(jnp.where(hot2, rank, 0.0), axis=-1, keepdims=True)
    run = run_ref[...] + jnp.sum(hot, axis=0, keepdims=True)
    run_ref[...] = run
    cnt_ref[...] = run

    meta = jnp.zeros((tm, LANES), F32)
    for k, v in enumerate((i1, i2, rank1, rank2, gate1, gate2)):
        meta = jnp.where(lane == k, v, meta)
    meta_ref[...] = meta


def _router(x2, g2, w_r, b_r):
    t = x2.shape[0]
    tm = TM_ROUTE
    return pl.pallas_call(
        _router_kernel,
        grid=(t // tm,),
        in_specs=[pl.BlockSpec((tm, D_MODEL), lambda i: (i, 0)),
                  pl.BlockSpec((1, D_MODEL), lambda i: (0, 0)),
                  pl.BlockSpec((D_MODEL, 2 * LANES), lambda i: (0, 0)),
                  pl.BlockSpec((1, 2 * LANES), lambda i: (0, 0))],
        out_specs=[pl.BlockSpec((tm, D_MODEL), lambda i: (i, 0)),
                   pl.BlockSpec((tm, LANES), lambda i: (i, 0)),
                   pl.BlockSpec((1, LANES), lambda i: (0, 0))],
        out_shape=[jax.ShapeDtypeStruct((t, D_MODEL), F32),
                   jax.ShapeDtypeStruct((t, LANES), F32),
                   jax.ShapeDtypeStruct((1, LANES), F32)],
        scratch_shapes=[pltpu.VMEM((1, LANES), F32)],
        compiler_params=_cparams(1),
        name="router",
    )(x2, g2, w_r, b_r)


def _dispatch_kernel(plan_ref, dest_ref, h_ref, xs_ref, zero_ref, zsem, sem):
    i = pl.program_id(0)
    tt = dest_ref.shape[2] // TOP_K
    rows = zero_ref.shape[0]
    p_rows = xs_ref.shape[0]

    def pad_block(e):
        return plan_ref[N_EXPERTS + e] - rows, plan_ref[e] > 0

    def tail_block(c):
        row0 = plan_ref[2 * N_EXPERTS - 1] + c * rows
        return row0, row0 < p_rows

    def for_zero_blocks(block, action):
        def body(c, carry):
            row0, live = block(c)

            @pl.when(live)
            def _():
                dst = xs_ref.at[pl.ds(pl.multiple_of(row0, rows), rows)]
                action(pltpu.make_async_copy(zero_ref, dst, zsem))
            return carry

        lax.fori_loop(0, N_EXPERTS, body, 0)

    @pl.when(i == 0)
    def _():
        zero_ref[...] = jnp.zeros_like(zero_ref)
        for block in (pad_block, tail_block):
            for_zero_blocks(block, lambda cp: cp.start())
        for block in (pad_block, tail_block):
            for_zero_blocks(block, lambda cp: cp.wait())

    def row_copy(tok, k):
        dst = dest_ref[0, 0, TOP_K * tok + k]
        return pltpu.make_async_copy(h_ref.at[pl.ds(i * tt + tok, 1)], xs_ref.at[pl.ds(dst, 1)], sem)

    def start_rows(tok, c):
        for k in range(TOP_K):
            row_copy(tok, k).start()
        return c

    def wait_rows(tok, c):
        for k in range(TOP_K):
            row_copy(tok, k).wait()
        return c

    lax.fori_loop(0, tt, start_rows, 0)
    lax.fori_loop(0, tt, wait_rows, 0)


def _dispatch(plan, dest, h2, p_rows):
    t = h2.shape[0]
    tt = TT_ROWS
    return pl.pallas_call(
        _dispatch_kernel,
        grid_spec=pltpu.PrefetchScalarGridSpec(
            num_scalar_prefetch=1,
            grid=(t // tt,),
            in_specs=[pl.BlockSpec((1, 1, TOP_K * tt), lambda i, plan: (i, 0, 0), memory_space=pltpu.SMEM),
                      pl.BlockSpec(memory_space=pl.ANY)],
            out_specs=pl.BlockSpec(memory_space=pl.ANY),
            scratch_shapes=[pltpu.VMEM((MOE_ROWS, D_MODEL), F32), pltpu.SemaphoreType.DMA(()),
                            pltpu.SemaphoreType.DMA(())]),
        out_shape=jax.ShapeDtypeStruct((p_rows, D_MODEL), F32),
        compiler_params=_cparams(1),
        name="dispatch",
    )(plan, dest, h2)


def _experts_kernel(blk_ref, nused_ref, xs_ref, wg_ref, wu_ref, wd_ref, ys_ref):
    i = pl.program_id(0)

    @pl.when(i < nused_ref[0])
    def _():
        x = xs_ref[...].astype(BF16)
        g = jnp.dot(x, wg_ref[...], preferred_element_type=F32)
        u = jnp.dot(x, wu_ref[...], preferred_element_type=F32)
        hid = g * (1.0 / (1.0 + jnp.exp(-g))) * u
        ys_ref[...] = jnp.dot(hid.astype(BF16), wd_ref[...], preferred_element_type=F32)

    @pl.when(i >= nused_ref[0])
    def _():
        ys_ref[...] = jnp.zeros_like(ys_ref)


def _experts(blk_exp, nused, xs, w_gate, w_up, w_down):
    p_rows = xs.shape[0]
    rows = MOE_ROWS
    live = lambda i, blk, nused: jnp.minimum(i, jnp.maximum(nused[0] - 1, 0))
    return pl.pallas_call(
        _experts_kernel,
        grid_spec=pltpu.PrefetchScalarGridSpec(
            num_scalar_prefetch=2,
            grid=(p_rows // rows,),
            in_specs=[pl.BlockSpec((rows, D_MODEL), lambda i, blk, nused: (live(i, blk, nused), 0)),
                      pl.BlockSpec((None, D_MODEL, D_EXPERT), lambda i, blk, nused: (blk[i], 0, 0)),
                      pl.BlockSpec((None, D_MODEL, D_EXPERT), lambda i, blk, nused: (blk[i], 0, 0)),
                      pl.BlockSpec((None, D_EXPERT, D_MODEL), lambda i, blk, nused: (blk[i], 0, 0))],
            out_specs=pl.BlockSpec((rows, D_MODEL), lambda i, blk, nused: (i, 0))),
        out_shape=jax.ShapeDtypeStruct((p_rows, D_MODEL), F32),
        compiler_params=_cparams(1),
        name="experts",
    )(blk_exp, nused, xs, w_gate, w_up, w_down)


def _combine_kernel(dest_ref, x_ref, meta_ref, ys_ref, o_ref, buf_ref, sem):
    tt = x_ref.shape[0]

    def row_copy(tok, k):
        src = dest_ref[0, 0, TOP_K * tok + k]
        return pltpu.make_async_copy(ys_ref.at[pl.ds(src, 1)], buf_ref.at[k, pl.ds(tok, 1)], sem)

    def start_rows(tok, c):
        for k in range(TOP_K):
            row_copy(tok, k).start()
        return c

    def wait_rows(tok, c):
        for k in range(TOP_K):
            row_copy(tok, k).wait()
        return c

    lax.fori_loop(0, tt, start_rows, 0)
    lax.fori_loop(0, tt, wait_rows, 0)
    meta = meta_ref[...]
    moe = meta[:, 4:5] * buf_ref[0] + meta[:, 5:6] * buf_ref[1]
    o_ref[...] = x_ref[...] + moe


def _combine(dest, x2, meta, ys):
    t = x2.shape[0]
    tt = TT_ROWS
    return pl.pallas_call(
        _combine_kernel,
        grid=(t // tt,),
        in_specs=[pl.BlockSpec((1, 1, TOP_K * tt), lambda i: (i, 0, 0), memory_space=pltpu.SMEM),
                  pl.BlockSpec((tt, D_MODEL), lambda i: (i, 0)),
                  pl.BlockSpec((tt, LANES), lambda i: (i, 0)),
                  pl.BlockSpec(memory_space=pl.ANY)],
        out_specs=pl.BlockSpec((tt, D_MODEL), lambda i: (i, 0)),
        out_shape=jax.ShapeDtypeStruct((t, D_MODEL), F32),
        scratch_shapes=[pltpu.VMEM((TOP_K, tt, D_MODEL), F32), pltpu.SemaphoreType.DMA(())],
        compiler_params=_cparams(1),
        name="combine",
    )(dest, x2, meta, ys)


def _rope_tables(s):
    rows = s // GRID_W
    r, c = jnp.meshgrid(jnp.arange(rows, dtype=F32), jnp.arange(GRID_W, dtype=F32), indexing='ij')
    n_freq = HEAD_DIM // 4
    inv = ROPE_THETA ** (-jnp.arange(n_freq, dtype=F32) / n_freq)
    ang_r = r.reshape(-1)[:, None] * inv
    ang_c = c.reshape(-1)[:, None] * inv
    ang = jnp.concatenate([ang_r, ang_r, ang_c, ang_c], axis=-1)
    sign = np.tile(np.repeat(np.array([-1.0, 1.0], np.float32), n_freq), 2)
    return jnp.cos(ang), jnp.sin(ang) * sign


def _moe_plan(meta, cnt, t):
    rows = MOE_ROWS
    p_rows = TOP_K * t + N_EXPERTS * rows
    nblk = p_rows // rows
    counts = cnt[0, :N_EXPERTS].astype(jnp.int32)
    pcounts = (counts + rows - 1) // rows * rows
    pends = jnp.cumsum(pcounts)
    pstarts = pends - pcounts
    expert = meta[:, 0:TOP_K].astype(jnp.int32)
    rank = meta[:, TOP_K:2 * TOP_K].astype(jnp.int32)
    dest = (pstarts[expert] + rank).reshape(t // TT_ROWS, 1, TOP_K * TT_ROWS)
    nused = pends[-1] // rows
    blk = jnp.minimum(jnp.searchsorted(pends, jnp.arange(nblk, dtype=jnp.int32) * rows, side='right'),
                      N_EXPERTS - 1).astype(jnp.int32)
    blk = jnp.where(jnp.arange(nblk) < nused, blk, blk[jnp.maximum(nused - 1, 0)])
    plan = jnp.concatenate([pcounts, pends]).astype(jnp.int32)
    return plan, dest, blk, nused.reshape(1).astype(jnp.int32), p_rows


def _trunk(x, p):
    b, s, _ = x.shape
    t = b * s
    assert s % (TQ_DIL * max(d for _, d in DILATED)) == 0 and s % TM_IN == 0 and t % TT_ROWS == 0
    x = x.reshape(t, D_MODEL)
    cos, sin = _rope_tables(s)
    qkv = _in_proj(x, p['norm1_g'], p['w_in'], p['gain'], cos, sin, s)
    branches = [_dilated_branch(qkv, p['band'][i], b, s, d) for i, (_, d) in enumerate(DILATED)]
    ob = _gqa(qkv, b, s)
    x2 = _out_proj([o for o, _ in branches], [l for _, l in branches], ob, x, p['on_a'], p['on_b'], p['w_out'])
    h2, meta, cnt = _router(x2, p['norm2_g'], p['w_r'], p['b_r'])
    plan, dest, blk, nused, p_rows = _moe_plan(meta, cnt, t)
    xs = _dispatch(plan, dest, h2, p_rows)
    ys = _experts(blk, nused, xs, p['w_gate'], p['w_up'], p['w_down'])
    y = _combine(dest, x2, meta, ys)
    return y.reshape(b, s, D_MODEL)


def kernel(x_prompt, x_sample, norm1_g, w_in, qn_a, kn_a, qn_b, kn_b, rel_bias, on_a, on_b, w_out, norm2_g, rg_w,
           rg_b, re_w, re_b, w_gate, w_up, w_down):
    assert norm1_g.shape[0] == 1, "one layer"
    scale = HEAD_DIM ** -0.5
    ones_a = jnp.ones((WIDTH_A,), F32)
    gain = jnp.concatenate([jnp.tile(qn_a[0], N_HEADS_A) * scale, jnp.tile(kn_a[0], N_HEADS_A), ones_a,
                            jnp.tile(qn_b[0], N_HEADS_B) * scale, jnp.tile(kn_b[0], N_KV_B),
                            jnp.ones((KV_WIDTH_B,), F32)]).reshape(1, IN_WIDTH)
    pad_g = jnp.zeros((D_MODEL, LANES - N_GROUPS), F32)
    pad_e = jnp.zeros((D_MODEL, LANES - N_EXPERTS), F32)
    params = dict(
        norm1_g=norm1_g, w_in=w_in[0].astype(BF16), gain=gain,
        band=[_band_tables(rel_bias, d) for _, d in DILATED],
        on_a=on_a, on_b=on_b, w_out=w_out[0].astype(BF16), norm2_g=norm2_g,
        w_r=jnp.concatenate([rg_w[0], pad_g, re_w[0], pad_e], axis=1).astype(BF16),
        b_r=jnp.concatenate([rg_b[0], jnp.zeros((LANES - N_GROUPS,), F32), re_b[0],
                             jnp.zeros((LANES - N_EXPERTS,), F32)]).reshape(1, 2 * LANES),
        w_gate=w_gate[0].astype(BF16), w_up=w_up[0].astype(BF16), w_down=w_down[0].astype(BF16))
    return _trunk(x_prompt, params), _trunk(x_sample, params)
```

```python
import functools

import numpy as np
import jax
import jax.numpy as jnp
from jax import lax
from jax.experimental import pallas as pl
from jax.experimental.pallas import tpu as pltpu

D_MODEL = 2048
HEAD_DIM = 128
N_HEADS_A = 8
N_HEADS_B = 8
N_KV_B = 2
WIDTH_A = N_HEADS_A * HEAD_DIM
WIDTH_B = N_HEADS_B * HEAD_DIM
KV_WIDTH_B = N_KV_B * HEAD_DIM
IN_WIDTH = 3 * WIDTH_A + WIDTH_B + 2 * KV_WIDTH_B
DILATED = ((128, 1), (512, 4), (2048, 16))
GRID_W = 64
ROPE_THETA = 10000.0
NUM_BUCKETS = 32
MAX_DISTANCE = 1024
N_GROUPS = 4
EXPERTS_PER_GROUP = 8
N_EXPERTS = N_GROUPS * EXPERTS_PER_GROUP
TOP_K = 2
D_EXPERT = D_MODEL // 2
EPS = 1e-6
NEG_INF = -1e30

LANES = 128
SEC = 4 * HEAD_DIM
N_SEC = IN_WIDTH // SEC
SEC_QA, SEC_KA, SEC_VA, SEC_QB, SEC_KVB = 0, 2, 4, 6, 8
BAND = 64
VMEM_LIMIT = 56 * 1024 * 1024

TM_IN = 512
TQ_DIL = 128
TQ_GQA = 128
TK_GQA = 512
TM_OUT = 256
TM_ROUTE = 256
TT_ROWS = 256
MOE_ROWS = 256

F32 = jnp.float32
BF16 = jnp.bfloat16


def _cparams(n_axes):
    return pltpu.CompilerParams(dimension_semantics=("arbitrary",) * n_axes, vmem_limit_bytes=VMEM_LIMIT)


def _in_proj_kernel(x_ref, g1_ref, w_ref, gain_ref, cos_ref, sin_ref, o_ref, h_ref):
    j = pl.program_id(1)

    @pl.when(j == 0)
    def _():
        x = x_ref[...]
        ms = jnp.mean(x * x, axis=-1, keepdims=True)
        h_ref[...] = (x * lax.rsqrt(ms + EPS) * g1_ref[...]).astype(BF16)

    p = jnp.dot(h_ref[...], w_ref[...], preferred_element_type=F32)
    lane = lax.broadcasted_iota(jnp.int32, (p.shape[0], HEAD_DIM), 1)
    first_quarter = (lane & 32) == 0

    def emit(norm, rope):
        for hh in range(SEC // HEAD_DIM):
            sl = slice(hh * HEAD_DIM, (hh + 1) * HEAD_DIM)
            y = p[:, sl]
            if norm[hh]:
                ms = jnp.mean(y * y, axis=-1, keepdims=True)
                y = y * lax.rsqrt(ms + EPS)
            y = y * gain_ref[:, sl]
            if rope[hh]:
                partner = jnp.where(first_quarter, pltpu.roll(y, HEAD_DIM - 32, 1), pltpu.roll(y, 32, 1))
                y = y * cos_ref[...] + partner * sin_ref[...]
            o_ref[:, sl] = y.astype(BF16)

    yes, no = (True,) * 4, (False,) * 4

    @pl.when(j < SEC_VA)
    def _():
        emit(yes, no)

    @pl.when((j >= SEC_VA) & (j < SEC_QB))
    def _():
        emit(no, no)

    @pl.when((j >= SEC_QB) & (j < SEC_KVB))
    def _():
        emit(yes, yes)

    @pl.when(j == SEC_KVB)
    def _():
        emit((True, True, False, False), (True, True, False, False))


def _in_proj(x, g1, w_in, gain, cos, sin, s):
    t = x.shape[0]
    tm = TM_IN
    spt = s // tm
    return pl.pallas_call(
        _in_proj_kernel,
        grid=(t // tm, N_SEC),
        in_specs=[
            pl.BlockSpec((tm, D_MODEL), lambda i, j: (i, 0)),
            pl.BlockSpec((1, D_MODEL), lambda i, j: (0, 0)),
            pl.BlockSpec((D_MODEL, SEC), lambda i, j: (0, j)),
            pl.BlockSpec((1, SEC), lambda i, j: (0, j)),
            pl.BlockSpec((tm, HEAD_DIM), lambda i, j: (i % spt, 0)),
            pl.BlockSpec((tm, HEAD_DIM), lambda i, j: (i % spt, 0)),
        ],
        out_specs=pl.BlockSpec((None, tm, SEC), lambda i, j: (j, i, 0)),
        out_shape=jax.ShapeDtypeStruct((N_SEC, t, SEC), BF16),
        scratch_shapes=[pltpu.VMEM((tm, D_MODEL), BF16)],
        compiler_params=_cparams(2),
        name="in_proj",
    )(x, g1, w_in, gain, cos, sin)


def _dilated_kernel(q_ref, kp_ref, kc_ref, kn_ref, vp_ref, vc_ref, vn_ref, a_ref, o_ref, lse_ref):
    tq = q_ref.shape[0]
    lane = lax.broadcasted_iota(jnp.int32, (tq, LANES), 1)
    lse_all = jnp.zeros((tq, LANES), F32)
    nt = (((1,), (1,)), ((), ()))
    for hh in range(SEC // HEAD_DIM):
        sl = slice(hh * HEAD_DIM, (hh + 1) * HEAD_DIM)
        q = q_ref[:, sl]
        s = jnp.concatenate(
            [lax.dot_general(q, k_ref[:, sl], nt, preferred_element_type=F32) for k_ref in (kp_ref, kc_ref, kn_ref)],
            axis=1) + a_ref[hh]
        m = jnp.max(s, axis=-1, keepdims=True)
        p = jnp.exp(s - m)
        l = jnp.sum(p, axis=-1, keepdims=True)
        pb = p.astype(BF16)
        o = (jnp.dot(pb[:, :tq], vp_ref[:, sl], preferred_element_type=F32)
             + jnp.dot(pb[:, tq:2 * tq], vc_ref[:, sl], preferred_element_type=F32)
             + jnp.dot(pb[:, 2 * tq:], vn_ref[:, sl], preferred_element_type=F32))
        o_ref[:, sl] = o / l
        lse_all = jnp.where(lane == hh, m + jnp.log(l), lse_all)
    lse_ref[...] = lse_all


def _dilated_branch(qkv, a_tab, b, s, d):
    ln = s // d
    tq = TQ_DIL
    nq = ln // tq
    view = qkv.reshape(N_SEC, b, ln, d * SEC)

    def variant(qi):
        if nq == 1:
            return 3
        return jnp.where(qi == 0, 0, jnp.where(qi == nq - 1, 2, 1))

    def kv_spec(sec, shift):
        return pl.BlockSpec((None, None, tq, SEC),
                            lambda bi, r, qi, hg: (sec + hg, bi, jnp.clip(qi + shift, 0, nq - 1), r))

    o, lse = pl.pallas_call(
        _dilated_kernel,
        grid=(b, d, nq, 2),
        in_specs=[pl.BlockSpec((None, None, tq, SEC), lambda bi, r, qi, hg: (SEC_QA + hg, bi, qi, r)),
                  kv_spec(SEC_KA, -1), kv_spec(SEC_KA, 0), kv_spec(SEC_KA, 1),
                  kv_spec(SEC_VA, -1), kv_spec(SEC_VA, 0), kv_spec(SEC_VA, 1),
                  pl.BlockSpec((None, None, 4, tq, 3 * tq), lambda bi, r, qi, hg: (variant(qi), hg, 0, 0, 0))],
        out_specs=[pl.BlockSpec((None, tq, SEC), lambda bi, r, qi, hg: (bi, qi, 2 * r + hg)),
                   pl.BlockSpec((None, None, tq, LANES), lambda bi, r, qi, hg: (hg, bi, qi, r))],
        out_shape=[jax.ShapeDtypeStruct((b, ln, d * WIDTH_A), F32),
                   jax.ShapeDtypeStruct((2, b, ln, d * LANES), F32)],
        compiler_params=_cparams(4),
        name=f"dilated_d{d}",
    )(view, view, view, view, view, view, view, a_tab)
    return o.reshape(b * s, WIDTH_A), lse.reshape(2, b * s, LANES)


def _t5_buckets(rel):
    nb = NUM_BUCKETS // 2
    max_exact = nb // 2
    n = np.abs(rel)
    large = max_exact + (np.log(np.maximum(n, 1) / max_exact) / np.log(MAX_DISTANCE / max_exact)
                         * (nb - max_exact)).astype(np.int32)
    large = np.minimum(large, nb - 1)
    return (rel > 0).astype(np.int32) * nb + np.where(n < max_exact, n, large).astype(np.int32)


def _band_tables(rel_bias, d):
    tq = TQ_DIL
    width = 3 * tq
    rel = np.arange(width + tq - 1) - (tq - 1) - tq
    bucket = _t5_buckets(d * np.clip(rel, -BAND, BAND))
    diag = jnp.where(np.abs(rel) <= BAND, rel_bias.astype(F32)[bucket].T, NEG_INF)
    n = diag.shape[1]
    flat = jnp.tile(diag, (1, tq))[:, tq - 1:tq - 1 + tq * (n - 1)]
    table = flat.reshape(N_HEADS_A, tq, n - 1)[:, :, :width]
    c = np.arange(width)[None, None, :]
    prev_ok, next_ok = c >= tq, c < 2 * tq
    tabs = [jnp.where(keep, table, NEG_INF) for keep in (prev_ok, c >= 0, next_ok, prev_ok & next_ok)]
    return jnp.stack(tabs).reshape(4, 2, 4, tq, width)


def _gqa_kernel(q_ref, k_ref, v_ref, o_ref, *, tk):
    tq = q_ref.shape[0]
    nk = k_ref.shape[0] // tk
    rep = SEC // HEAD_DIM
    q = jnp.concatenate([q_ref[:, hh * HEAD_DIM:(hh + 1) * HEAD_DIM] for hh in range(rep)], axis=0)
    nt = (((1,), (1,)), ((), ()))

    def body(c, carry):
        m, l, acc = carry
        start = pl.multiple_of(c * tk, tk)
        s = lax.dot_general(q, k_ref[pl.ds(start, tk), :], nt, preferred_element_type=F32)
        m_new = jnp.maximum(m, jnp.max(s, axis=-1, keepdims=True))
        alpha = jnp.exp(m - m_new)
        p = jnp.exp(s - m_new)
        l = alpha * l + jnp.sum(p, axis=-1, keepdims=True)
        acc = alpha * acc + jnp.dot(p.astype(BF16), v_ref[pl.ds(start, tk), :], preferred_element_type=F32)
        return m_new, l, acc

    init = (jnp.full((rep * tq, 1), NEG_INF, F32), jnp.zeros((rep * tq, 1), F32),
            jnp.zeros((rep * tq, HEAD_DIM), F32))
    _, l, acc = lax.fori_loop(0, nk, body, init)
    o = acc / l
    for hh in range(rep):
        o_ref[:, hh * HEAD_DIM:(hh + 1) * HEAD_DIM] = o[hh * tq:(hh + 1) * tq]


def _gqa(qkv, b, s):
    tq = TQ_GQA
    view = qkv.reshape(N_SEC, b, s, SEC)
    o = pl.pallas_call(
        functools.partial(_gqa_kernel, tk=min(TK_GQA, s)),
        grid=(b, N_KV_B, s // tq),
        in_specs=[pl.BlockSpec((None, None, tq, SEC), lambda bi, g, qi: (SEC_QB + g, bi, qi, 0)),
                  pl.BlockSpec((None, None, s, HEAD_DIM), lambda bi, g, qi: (SEC_KVB, bi, 0, g)),
                  pl.BlockSpec((None, None, s, HEAD_DIM), lambda bi, g, qi: (SEC_KVB, bi, 0, N_KV_B + g))],
        out_specs=pl.BlockSpec((None, tq, SEC), lambda bi, g, qi: (bi, qi, g)),
        out_shape=jax.ShapeDtypeStruct((b, s, WIDTH_B), F32),
        compiler_params=_cparams(3),
        name="gqa",
    )(view, view, view)
    return o.reshape(b * s, WIDTH_B)


def _out_proj_kernel(o1_ref, o2_ref, o3_ref, l1_ref, l2_ref, l3_ref, ob_ref, x_ref, ga_ref, gb_ref, w_ref, y_ref):
    lses = (l1_ref[...], l2_ref[...], l3_ref[...])
    m = jnp.maximum(jnp.maximum(lses[0], lses[1]), lses[2])
    es = [jnp.exp(v - m) for v in lses]
    tot = es[0] + es[1] + es[2]
    ws = [e / tot for e in es]
    heads = []
    for h in range(N_HEADS_A):
        sl = slice(h * HEAD_DIM, (h + 1) * HEAD_DIM)
        hg, hl = divmod(h, 4)
        acc = None
        for w, o_ref in zip(ws, (o1_ref, o2_ref, o3_ref)):
            term = w[hg][:, hl:hl + 1] * o_ref[:, sl]
            acc = term if acc is None else acc + term
        heads.append(acc)
    oa = jnp.concatenate(heads, axis=1)
    oa = oa * lax.rsqrt(jnp.mean(oa * oa, axis=-1, keepdims=True) + EPS) * ga_ref[...]
    ob = ob_ref[...]
    ob = ob * lax.rsqrt(jnp.mean(ob * ob, axis=-1, keepdims=True) + EPS) * gb_ref[...]
    y = (jnp.dot(oa.astype(BF16), w_ref[:WIDTH_A, :], preferred_element_type=F32)
         + jnp.dot(ob.astype(BF16), w_ref[WIDTH_A:, :], preferred_element_type=F32))
    y_ref[...] = x_ref[...] + y


def _out_proj(os_, lses, ob, x, on_a, on_b, w_out):
    t = x.shape[0]
    tm = TM_OUT
    row = lambda width: pl.BlockSpec((tm, width), lambda i: (i, 0))
    lse_spec = pl.BlockSpec((2, tm, LANES), lambda i: (0, i, 0))
    const = lambda shape: pl.BlockSpec(shape, lambda i: (0, 0))
    return pl.pallas_call(
        _out_proj_kernel,
        grid=(t // tm,),
        in_specs=[row(WIDTH_A), row(WIDTH_A), row(WIDTH_A), lse_spec, lse_spec, lse_spec, row(WIDTH_B),
                  row(D_MODEL), const((1, WIDTH_A)), const((1, WIDTH_B)), const((WIDTH_A + WIDTH_B, D_MODEL))],
        out_specs=row(D_MODEL),
        out_shape=jax.ShapeDtypeStruct((t, D_MODEL), F32),
        compiler_params=_cparams(1),
        name="out_proj",
    )(*os_, *lses, ob, x, on_a, on_b, w_out)


def _router_kernel(x_ref, g2_ref, w_ref, b_ref, h_ref, meta_ref, cnt_ref, run_ref):
    tm = x_ref.shape[0]

    @pl.when(pl.program_id(0) == 0)
    def _():
        run_ref[...] = jnp.zeros_like(run_ref)

    x = x_ref[...]
    h = x * lax.rsqrt(jnp.mean(x * x, axis=-1, keepdims=True) + EPS) * g2_ref[...]
    h_ref[...] = h
    logits = jnp.dot(h.astype(BF16), w_ref[...], preferred_element_type=F32) + b_ref[...]
    gl, el = logits[:, :LANES], logits[:, LANES:]
    lane = lax.broadcasted_iota(jnp.int32, (tm, LANES), 1).astype(F32)
    ninf = jnp.float32(-jnp.inf)
    big = jnp.float32(LANES)

    def first_argmax(v):
        top = jnp.max(v, axis=-1, keepdims=True)
        return top, jnp.min(jnp.where(v == top, lane, big), axis=-1, keepdims=True)

    gmask = lane < N_GROUPS
    gtop, gsel = first_argmax(jnp.where(gmask, gl, ninf))
    g_w = 1.0 / jnp.sum(jnp.where(gmask, jnp.exp(gl - gtop), 0.0), axis=-1, keepdims=True)
    lo = gsel * EXPERTS_PER_GROUP
    elm = jnp.where((lane >= lo) & (lane < lo + EXPERTS_PER_GROUP), el, ninf)
    t1, i1 = first_argmax(elm)
    t2, i2 = first_argmax(jnp.where(lane == i1, ninf, elm))
    e2 = jnp.exp(t2 - t1)
    gate1 = g_w * (1.0 / (1.0 + e2))
    gate2 = g_w * (e2 / (1.0 + e2))

    hot1, hot2 = lane == i1, lane == i2
    hot = (hot1 | hot2).astype(F32)
    r_i = lax.broadcasted_iota(jnp.int32, (tm, tm), 0)
    c_i = lax.broadcasted_iota(jnp.int32, (tm, tm), 1)
    before = (c_i < r_i).astype(BF16)
    rank = jnp.dot(before, hot.astype(BF16), preferred_element_type=F32) + run_ref[...]
    rank1 = jnp.sum(jnp.where(hot1, rank, 0.0), axis=-1, keepdims=True)
    rank2 = jnp.sum(jnp.where(hot2, rank, 0.0), axis=-1, keepdims=True)
    run = run_ref[...] + jnp.sum(hot, axis=0, keepdims=True)
    run_ref[...] = run
    cnt_ref[...] = run

    meta = jnp.zeros((tm, LANES), F32)
    for k, v in enumerate((i1, i2, rank1, rank2, gate1, gate2)):
        meta = jnp.where(lane == k, v, meta)
    meta_ref[...] = meta


def _router(x2, g2, w_r, b_r):
    t = x2.shape[0]
    tm = TM_ROUTE
    return pl.pallas_call(
        _router_kernel,
        grid=(t // tm,),
        in_specs=[pl.BlockSpec((tm, D_MODEL), lambda i: (i, 0)),
                  pl.BlockSpec((1, D_MODEL), lambda i: (0, 0)),
                  pl.BlockSpec((D_MODEL, 2 * LANES), lambda i: (0, 0)),
                  pl.BlockSpec((1, 2 * LANES), lambda i: (0, 0))],
        out_specs=[pl.BlockSpec((tm, D_MODEL), lambda i: (i, 0)),
                   pl.BlockSpec((tm, LANES), lambda i: (i, 0)),
                   pl.BlockSpec((1, LANES), lambda i: (0, 0))],
        out_shape=[jax.ShapeDtypeStruct((t, D_MODEL), F32),
                   jax.ShapeDtypeStruct((t, LANES), F32),
                   jax.ShapeDtypeStruct((1, LANES), F32)],
        scratch_shapes=[pltpu.VMEM((1, LANES), F32)],
        compiler_params=_cparams(1),
        name="router",
    )(x2, g2, w_r, b_r)


def _dispatch_kernel(plan_ref, dest_ref, h_ref, xs_ref, zero_ref, zsem, sem):
    i = pl.program_id(0)
    tt = dest_ref.shape[2] // TOP_K
    rows = zero_ref.shape[0]
    p_rows = xs_ref.shape[0]

    def pad_block(e):
        return plan_ref[N_EXPERTS + e] - rows, plan_ref[e] > 0

    def tail_block(c):
        row0 = plan_ref[2 * N_EXPERTS - 1] + c * rows
        return row0, row0 < p_rows

    def for_zero_blocks(block, action):
        def body(c, carry):
            row0, live = block(c)

            @pl.when(live)
            def _():
                dst = xs_ref.at[pl.ds(pl.multiple_of(row0, rows), rows)]
                action(pltpu.make_async_copy(zero_ref, dst, zsem))
            return carry

        lax.fori_loop(0, N_EXPERTS, body, 0)

    @pl.when(i == 0)
    def _():
        zero_ref[...] = jnp.zeros_like(zero_ref)
        for block in (pad_block, tail_block):
            for_zero_blocks(block, lambda cp: cp.start())
        for block in (pad_block, tail_block):
            for_zero_blocks(block, lambda cp: cp.wait())

    def row_copy(tok, k):
        dst = dest_ref[0, 0, TOP_K * tok + k]
        return pltpu.make_async_copy(h_ref.at[pl.ds(tok, 1)], xs_ref.at[pl.ds(dst, 1)], sem)

    def start_rows(tok, c):
        for k in range(TOP_K):
            row_copy(tok, k).start()
        return c

    def wait_rows(tok, c):
        for k in range(TOP_K):
            row_copy(tok, k).wait()
        return c

    lax.fori_loop(0, tt, start_rows, 0)
    lax.fori_loop(0, tt, wait_rows, 0)


def _dispatch(plan, dest, h2, p_rows):
    t = h2.shape[0]
    tt = TT_ROWS
    return pl.pallas_call(
        _dispatch_kernel,
        grid_spec=pltpu.PrefetchScalarGridSpec(
            num_scalar_prefetch=1,
            grid=(t // tt,),
            in_specs=[pl.BlockSpec((1, 1, TOP_K * tt), lambda i, plan: (i, 0, 0), memory_space=pltpu.SMEM),
                      pl.BlockSpec((tt, D_MODEL), lambda i, plan: (i, 0))],
            out_specs=pl.BlockSpec(memory_space=pl.ANY),
            scratch_shapes=[pltpu.VMEM((MOE_ROWS, D_MODEL), F32), pltpu.SemaphoreType.DMA(()),
                            pltpu.SemaphoreType.DMA(())]),
        out_shape=jax.ShapeDtypeStruct((p_rows, D_MODEL), F32),
        compiler_params=_cparams(1),
        name="dispatch",
    )(plan, dest, h2)


def _experts_kernel(blk_ref, nused_ref, xs_ref, wg_ref, wu_ref, wd_ref, ys_ref):
    i = pl.program_id(0)

    @pl.when(i < nused_ref[0])
    def _():
        x = xs_ref[...].astype(BF16)
        g = jnp.dot(x, wg_ref[...], preferred_element_type=F32)
        u = jnp.dot(x, wu_ref[...], preferred_element_type=F32)
        hid = g * (1.0 / (1.0 + jnp.exp(-g))) * u
        ys_ref[...] = jnp.dot(hid.astype(BF16), wd_ref[...], preferred_element_type=F32)

    @pl.when(i >= nused_ref[0])
    def _():
        ys_ref[...] = jnp.zeros_like(ys_ref)


def _experts(blk_exp, nused, xs, w_gate, w_up, w_down):
    p_rows = xs.shape[0]
    rows = MOE_ROWS
    live = lambda i, blk, nused: jnp.minimum(i, jnp.maximum(nused[0] - 1, 0))
    return pl.pallas_call(
        _experts_kernel,
        grid_spec=pltpu.PrefetchScalarGridSpec(
            num_scalar_prefetch=2,
            grid=(p_rows // rows,),
            in_specs=[pl.BlockSpec((rows, D_MODEL), lambda i, blk, nused: (live(i, blk, nused), 0)),
                      pl.BlockSpec((None, D_MODEL, D_EXPERT), lambda i, blk, nused: (blk[i], 0, 0)),
                      pl.BlockSpec((None, D_MODEL, D_EXPERT), lambda i, blk, nused: (blk[i], 0, 0)),
                      pl.BlockSpec((None, D_EXPERT, D_MODEL), lambda i, blk, nused: (blk[i], 0, 0))],
            out_specs=pl.BlockSpec((rows, D_MODEL), lambda i, blk, nused: (i, 0))),
        out_shape=jax.ShapeDtypeStruct((p_rows, D_MODEL), F32),
        compiler_params=_cparams(1),
        name="experts",
    )(blk_exp, nused, xs, w_gate, w_up, w_down)


def _combine_kernel(dest_ref, x_ref, meta_ref, ys_ref, o_ref, buf_ref, sem):
    tt = x_ref.shape[0]

    def row_copy(tok, k):
        src = dest_ref[0, 0, TOP_K * tok + k]
        return pltpu.make_async_copy(ys_ref.at[pl.ds(src, 1)], buf_ref.at[k, pl.ds(tok, 1)], sem)

    def start_rows(tok, c):
        for k in range(TOP_K):
            row_copy(tok, k).start()
        return c

    def wait_rows(tok, c):
        for k in range(TOP_K):
            row_copy(tok, k).wait()
        return c

    lax.fori_loop(0, tt, start_rows, 0)
    lax.fori_loop(0, tt, wait_rows, 0)
    meta = meta_ref[...]
    moe = meta[:, 4:5] * buf_ref[0] + meta[:, 5:6] * buf_ref[1]
    o_ref[...] = x_ref[...] + moe


def _combine(dest, x2, meta, ys):
    t = x2.shape[0]
    tt = TT_ROWS
    return pl.pallas_call(
        _combine_kernel,
        grid=(t // tt,),
        in_specs=[pl.BlockSpec((1, 1, TOP_K * tt), lambda i: (i, 0, 0), memory_space=pltpu.SMEM),
                  pl.BlockSpec((tt, D_MODEL), lambda i: (i, 0)),
                  pl.BlockSpec((tt, LANES), lambda i: (i, 0)),
                  pl.BlockSpec(memory_space=pl.ANY)],
        out_specs=pl.BlockSpec((tt, D_MODEL), lambda i: (i, 0)),
        out_shape=jax.ShapeDtypeStruct((t, D_MODEL), F32),
        scratch_shapes=[pltpu.VMEM((TOP_K, tt, D_MODEL), F32), pltpu.SemaphoreType.DMA(())],
        compiler_params=_cparams(1),
        name="combine",
    )(dest, x2, meta, ys)


def _rope_tables(s):
    rows = s // GRID_W
    r, c = jnp.meshgrid(jnp.arange(rows, dtype=F32), jnp.arange(GRID_W, dtype=F32), indexing='ij')
    n_freq = HEAD_DIM // 4
    inv = ROPE_THETA ** (-jnp.arange(n_freq, dtype=F32) / n_freq)
    ang_r = r.reshape(-1)[:, None] * inv
    ang_c = c.reshape(-1)[:, None] * inv
    ang = jnp.concatenate([ang_r, ang_r, ang_c, ang_c], axis=-1)
    sign = np.tile(np.repeat(np.array([-1.0, 1.0], np.float32), n_freq), 2)
    return jnp.cos(ang), jnp.sin(ang) * sign


def _moe_plan(meta, cnt, t):
    rows = MOE_ROWS
    p_rows = TOP_K * t + N_EXPERTS * rows
    nblk = p_rows // rows
    counts = cnt[0, :N_EXPERTS].astype(jnp.int32)
    pcounts = (counts + rows - 1) // rows * rows
    pends = jnp.cumsum(pcounts)
    pstarts = pends - pcounts
    expert = meta[:, 0:TOP_K].astype(jnp.int32)
    rank = meta[:, TOP_K:2 * TOP_K].astype(jnp.int32)
    ids = jnp.arange(N_EXPERTS, dtype=jnp.int32)
    start = jnp.sum(jnp.where(expert[..., None] == ids, pstarts, 0), axis=-1)
    dest = (start + rank).reshape(t // TT_ROWS, 1, TOP_K * TT_ROWS)
    nused = pends[-1] // rows
    row0 = jnp.arange(nblk, dtype=jnp.int32) * rows
    blk = jnp.sum(pends[None, :] <= jnp.minimum(row0, pends[-1] - rows)[:, None], axis=1).astype(jnp.int32)
    plan = jnp.concatenate([pcounts, pends]).astype(jnp.int32)
    return plan, dest, blk, nused.reshape(1).astype(jnp.int32), p_rows


def _trunk(x, p):
    b, s, _ = x.shape
    t = b * s
    assert s % (TQ_DIL * max(d for _, d in DILATED)) == 0 and s % TM_IN == 0 and t % TT_ROWS == 0
    x = x.reshape(t, D_MODEL)
    cos, sin = _rope_tables(s)
    qkv = _in_proj(x, p['norm1_g'], p['w_in'], p['gain'], cos, sin, s)
    branches = [_dilated_branch(qkv, p['band'][i], b, s, d) for i, (_, d) in enumerate(DILATED)]
    ob = _gqa(qkv, b, s)
    x2 = _out_proj([o for o, _ in branches], [l for _, l in branches], ob, x, p['on_a'], p['on_b'], p['w_out'])
    h2, meta, cnt = _router(x2, p['norm2_g'], p['w_r'], p['b_r'])
    plan, dest, blk, nused, p_rows = _moe_plan(meta, cnt, t)
    xs = _dispatch(plan, dest, h2, p_rows)
    ys = _experts(blk, nused, xs, p['w_gate'], p['w_up'], p['w_down'])
    y = _combine(dest, x2, meta, ys)
    return y.reshape(b, s, D_MODEL)


def kernel(x_prompt, x_sample, norm1_g, w_in, qn_a, kn_a, qn_b, kn_b, rel_bias, on_a, on_b, w_out, norm2_g, rg_w,
           rg_b, re_w, re_b, w_gate, w_up, w_down):
    assert norm1_g.shape[0] == 1, "one layer"
    scale = HEAD_DIM ** -0.5
    ones_a = jnp.ones((WIDTH_A,), F32)
    gain = jnp.concatenate([jnp.tile(qn_a[0], N_HEADS_A) * scale, jnp.tile(kn_a[0], N_HEADS_A), ones_a,
                            jnp.tile(qn_b[0], N_HEADS_B) * scale, jnp.tile(kn_b[0], N_KV_B),
                            jnp.ones((KV_WIDTH_B,), F32)]).reshape(1, IN_WIDTH)
    pad_g = jnp.zeros((D_MODEL, LANES - N_GROUPS), F32)
    pad_e = jnp.zeros((D_MODEL, LANES - N_EXPERTS), F32)
    params = dict(
        norm1_g=norm1_g, w_in=w_in[0].astype(BF16), gain=gain,
        band=[_band_tables(rel_bias, d) for _, d in DILATED],
        on_a=on_a, on_b=on_b, w_out=w_out[0].astype(BF16), norm2_g=norm2_g,
        w_r=jnp.concatenate([rg_w[0], pad_g, re_w[0], pad_e], axis=1).astype(BF16),
        b_r=jnp.concatenate([rg_b[0], jnp.zeros((LANES - N_GROUPS,), F32), re_b[0],
                             jnp.zeros((LANES - N_EXPERTS,), F32)]).reshape(1, 2 * LANES),
        w_gate=w_gate[0].astype(BF16), w_up=w_up[0].astype(BF16), w_down=w_down[0].astype(BF16))
    return _trunk(x_prompt, params), _trunk(x_sample, params)
```

```python
import functools

import numpy as np
import jax
import jax.numpy as jnp
from jax import lax
from jax.experimental import pallas as pl
from jax.experimental.pallas import tpu as pltpu

D_MODEL = 2048
HEAD_DIM = 128
N_HEADS_A = 8
N_HEADS_B = 8
N_KV_B = 2
WIDTH_A = N_HEADS_A * HEAD_DIM
WIDTH_B = N_HEADS_B * HEAD_DIM
KV_WIDTH_B = N_KV_B * HEAD_DIM
IN_WIDTH = 3 * WIDTH_A + WIDTH_B + 2 * KV_WIDTH_B
DILATED = ((128, 1), (512, 4), (2048, 16))
GRID_W = 64
ROPE_THETA = 10000.0
NUM_BUCKETS = 32
MAX_DISTANCE = 1024
N_GROUPS = 4
EXPERTS_PER_GROUP = 8
N_EXPERTS = N_GROUPS * EXPERTS_PER_GROUP
TOP_K = 2
D_EXPERT = D_MODEL // 2
EPS = 1e-6
NEG_INF = -1e30
LOG2E = float(np.log2(np.e))

LANES = 128
SEC = 4 * HEAD_DIM
N_SEC = IN_WIDTH // SEC
SEC_QA, SEC_KA, SEC_VA, SEC_QB, SEC_KVB = 0, 2, 4, 6, 8
BAND = 64
VMEM_LIMIT = 56 * 1024 * 1024

TM_IN = 512
TQ_DIL = 128
DIL_STEP = {1: (512, 1), 4: (256, 2), 16: (128, 4)}
TQ_GQA = 128
TK_GQA = 512
UNROLL_GQA = 16
TM_OUT = 256
TM_ROUTE = 256
TT_ROWS = 256
MOE_ROWS = 256

F32 = jnp.float32
BF16 = jnp.bfloat16


def _cparams(n_axes):
    return pltpu.CompilerParams(dimension_semantics=("arbitrary",) * n_axes, vmem_limit_bytes=VMEM_LIMIT)


def _in_proj_kernel(x_ref, g1_ref, w_ref, gain_ref, cos_ref, sin_ref, o_ref, r4_ref, r16_ref, h_ref, y_ref):
    j = pl.program_id(1)
    tm = x_ref.shape[0]

    @pl.when(j == 0)
    def _():
        x = x_ref[...]
        ms = jnp.mean(x * x, axis=-1, keepdims=True)
        h_ref[...] = (x * lax.rsqrt(ms + EPS) * g1_ref[...]).astype(BF16)

    p = jnp.dot(h_ref[...], w_ref[...], preferred_element_type=F32)
    lane = lax.broadcasted_iota(jnp.int32, (p.shape[0], HEAD_DIM), 1)
    first_quarter = (lane & 32) == 0

    def emit(norm, rope, residue_major=False):
        for hh in range(SEC // HEAD_DIM):
            sl = slice(hh * HEAD_DIM, (hh + 1) * HEAD_DIM)
            y = p[:, sl]
            if norm[hh]:
                ms = jnp.mean(y * y, axis=-1, keepdims=True)
                y = y * lax.rsqrt(ms + EPS)
            y = y * gain_ref[:, sl]
            if rope[hh]:
                partner = jnp.where(first_quarter, pltpu.roll(y, HEAD_DIM - 32, 1), pltpu.roll(y, 32, 1))
                y = y * cos_ref[...] + partner * sin_ref[...]
            o_ref[:, sl] = y.astype(BF16)
            if residue_major:
                y_ref[hh] = y
                for d, r_ref in ((4, r4_ref), (16, r16_ref)):
                    for r in range(d):
                        r_ref[r, :, sl] = y_ref[hh, pl.ds(r, tm // d, stride=d), :].astype(BF16)

    yes, no = (True,) * 4, (False,) * 4

    @pl.when(j < SEC_VA)
    def _():
        emit(yes, no, residue_major=True)

    @pl.when((j >= SEC_VA) & (j < SEC_QB))
    def _():
        emit(no, no, residue_major=True)

    @pl.when((j >= SEC_QB) & (j < SEC_KVB))
    def _():
        emit(yes, yes)

    @pl.when(j == SEC_KVB)
    def _():
        emit((True, True, False, False), (True, True, False, False))


def _in_proj(x, g1, w_in, gain, cos, sin, b, s):
    t = x.shape[0]
    tm = TM_IN
    spt = s // tm
    n_dil = SEC_QB

    def residue_major(d):
        spec = pl.BlockSpec((None, None, d, tm // d, SEC),
                            lambda i, j: (jnp.minimum(j, n_dil - 1), i // spt, 0, i % spt, 0))
        return spec, jax.ShapeDtypeStruct((n_dil, b, d, s // d, SEC), BF16)

    (r4_spec, r4_shape), (r16_spec, r16_shape) = residue_major(4), residue_major(16)
    return pl.pallas_call(
        _in_proj_kernel,
        grid=(t // tm, N_SEC),
        in_specs=[
            pl.BlockSpec((tm, D_MODEL), lambda i, j: (i, 0)),
            pl.BlockSpec((1, D_MODEL), lambda i, j: (0, 0)),
            pl.BlockSpec((D_MODEL, SEC), lambda i, j: (0, j)),
            pl.BlockSpec((1, SEC), lambda i, j: (0, j)),
            pl.BlockSpec((tm, HEAD_DIM), lambda i, j: (i % spt, 0)),
            pl.BlockSpec((tm, HEAD_DIM), lambda i, j: (i % spt, 0)),
        ],
        out_specs=[pl.BlockSpec((None, tm, SEC), lambda i, j: (j, i, 0)), r4_spec, r16_spec],
        out_shape=[jax.ShapeDtypeStruct((N_SEC, t, SEC), BF16), r4_shape, r16_shape],
        scratch_shapes=[pltpu.VMEM((tm, D_MODEL), BF16), pltpu.VMEM((SEC // HEAD_DIM, tm, HEAD_DIM), F32)],
        compiler_params=_cparams(2),
        name="in_proj",
    )(x, g1, w_in, gain, cos, sin)


def _dilated_kernel(q_ref, kp_ref, kc_ref, kn_ref, vp_ref, vc_ref, vn_ref, a_ref, o_ref, lse_ref, *, nq, unroll):
    d, rows = q_ref.shape[0], q_ref.shape[1]
    tq = TQ_DIL
    nb = rows // tq
    lane = lax.broadcasted_iota(jnp.int32, (tq, LANES), 1)
    nt = (((1,), (1,)), ((), ()))
    first, last = pl.program_id(1) == 0, pl.program_id(1) == nq - 1

    def variant(j):
        lo, hi = (first if j == 0 else False), (last if j == nb - 1 else False)
        if lo is False and hi is False:
            return 1
        if hi is False:
            return jnp.where(lo, 0, 1)
        if lo is False:
            return jnp.where(hi, 2, 1)
        return jnp.where(lo & hi, 3, jnp.where(lo, 0, jnp.where(hi, 2, 1)))

    def residue(r, carry):
        heads = range(SEC // HEAD_DIM)
        sls = [slice(hh * HEAD_DIM, (hh + 1) * HEAD_DIM) for hh in heads]
        ks = [jnp.concatenate([kp_ref[r, :, sl], kc_ref[r, :, sl], kn_ref[r, :, sl]], axis=0) for sl in sls]
        vs = [jnp.concatenate([vp_ref[r, :, sl], vc_ref[r, :, sl], vn_ref[r, :, sl]], axis=0) for sl in sls]
        for j in range(nb):
            out_rows = pl.ds(j * tq * d + r, tq, stride=d) if d > 1 else pl.ds(j * tq, tq)
            keys = slice(j * tq, (j + 1) * tq + 2 * BAND)
            var = variant(j)
            lse_all = jnp.zeros((tq, LANES), F32)
            for hh in heads:
                q = q_ref[r, j * tq:(j + 1) * tq, sls[hh]]
                s = lax.dot_general(q, ks[hh][keys], nt, preferred_element_type=F32) + a_ref[var, hh]
                m = jnp.max(s, axis=-1, keepdims=True)
                p = jnp.exp(s - m)
                l = jnp.sum(p, axis=-1, keepdims=True)
                o = jnp.dot(p.astype(BF16), vs[hh][keys], preferred_element_type=F32)
                o_ref[hh, out_rows, :] = o / l
                lse_all = jnp.where(lane == hh, m + jnp.log(l), lse_all)
            lse_ref[out_rows, :] = lse_all
        return carry

    if d == 1:
        residue(0, 0)
    else:
        lax.fori_loop(0, d, residue, 0, unroll=unroll)


def _dilated_branch(rd, a_tab, b, s, d):
    ln = s // d
    tq = TQ_DIL
    rows, unroll = DIL_STEP[d]
    rows = min(rows, ln)
    nq = ln // rows
    edge = rows // BAND

    def cur(sec):
        return pl.BlockSpec((None, None, d, rows, SEC), lambda bi, qi, hg: (sec + hg, bi, 0, qi, 0))

    def prev(sec):
        return pl.BlockSpec((None, None, d, BAND, SEC),
                            lambda bi, qi, hg: (sec + hg, bi, 0, jnp.maximum(edge * qi - 1, 0), 0))

    def nxt(sec):
        return pl.BlockSpec((None, None, d, BAND, SEC),
                            lambda bi, qi, hg: (sec + hg, bi, 0, jnp.minimum(edge * (qi + 1), edge * nq - 1), 0))

    return pl.pallas_call(
        functools.partial(_dilated_kernel, nq=nq, unroll=unroll),
        grid=(b, nq, 2),
        in_specs=[cur(SEC_QA), prev(SEC_KA), cur(SEC_KA), nxt(SEC_KA), prev(SEC_VA), cur(SEC_VA), nxt(SEC_VA),
                  pl.BlockSpec((4, None, 4, tq, tq + 2 * BAND), lambda bi, qi, hg: (0, hg, 0, 0, 0))],
        out_specs=[pl.BlockSpec((None, 4, d * rows, HEAD_DIM), lambda bi, qi, hg: (bi, hg, qi, 0)),
                   pl.BlockSpec((None, None, d * rows, LANES), lambda bi, qi, hg: (hg, bi, qi, 0))],
        out_shape=[jax.ShapeDtypeStruct((b, N_HEADS_A, s, HEAD_DIM), F32),
                   jax.ShapeDtypeStruct((2, b, s, LANES), F32)],
        compiler_params=_cparams(3),
        name=f"dilated_d{d}",
    )(rd, rd, rd, rd, rd, rd, rd, a_tab)


def _t5_buckets(rel):
    nb = NUM_BUCKETS // 2
    max_exact = nb // 2
    n = np.abs(rel)
    large = max_exact + (np.log(np.maximum(n, 1) / max_exact) / np.log(MAX_DISTANCE / max_exact)
                         * (nb - max_exact)).astype(np.int32)
    large = np.minimum(large, nb - 1)
    return (rel > 0).astype(np.int32) * nb + np.where(n < max_exact, n, large).astype(np.int32)


def _band_tables(rel_bias, d):
    tq = TQ_DIL
    width = tq + 2 * BAND
    rel = np.arange(width + tq - 1) - (tq - 1) - BAND
    bucket = _t5_buckets(d * np.clip(rel, -BAND, BAND))
    diag = jnp.where(np.abs(rel) <= BAND, rel_bias.astype(F32)[bucket].T, NEG_INF)
    n = diag.shape[1]
    flat = jnp.tile(diag, (1, tq))[:, tq - 1:tq - 1 + tq * (n - 1)]
    table = flat.reshape(N_HEADS_A, tq, n - 1)[:, :, :width]
    c = np.arange(width)[None, None, :]
    prev_ok, next_ok = c >= BAND, c < BAND + tq
    tabs = [jnp.where(keep, table, NEG_INF) for keep in (prev_ok, c >= 0, next_ok, prev_ok & next_ok)]
    return jnp.stack(tabs).reshape(4, 2, 4, tq, width)


def _gqa_kernel(q_ref, k_ref, v_ref, o_ref, vx_ref, *, tk, unroll):
    tq = q_ref.shape[0]
    nk = k_ref.shape[0] // tk
    rep = SEC // HEAD_DIM

    @pl.when(pl.program_id(2) == 0)
    def _():
        vx_ref[:, :HEAD_DIM] = v_ref[...]
        vx_ref[:, HEAD_DIM:] = jnp.ones((vx_ref.shape[0], HEAD_DIM), BF16)

    q = jnp.concatenate([q_ref[:, hh * HEAD_DIM:(hh + 1) * HEAD_DIM] for hh in range(rep)], axis=0)
    nt = (((1,), (1,)), ((), ()))

    def body(c, carry):
        m, l, acc = carry
        start = pl.multiple_of(c * tk, tk)
        s = lax.dot_general(q, k_ref[pl.ds(start, tk), :], nt, preferred_element_type=F32)
        m_new = jnp.maximum(m, jnp.max(s, axis=-1, keepdims=True))
        alpha = jnp.exp2(m - m_new)
        p = jnp.exp2((s - m_new).astype(BF16))
        pv = jnp.dot(p, vx_ref[pl.ds(start, tk), :], preferred_element_type=F32)
        l = alpha * l + pv[:, HEAD_DIM:HEAD_DIM + 1]
        acc = alpha * acc + pv[:, :HEAD_DIM]
        return m_new, l, acc

    init = (jnp.full((rep * tq, 1), NEG_INF, F32), jnp.zeros((rep * tq, 1), F32),
            jnp.zeros((rep * tq, HEAD_DIM), F32))
    _, l, acc = lax.fori_loop(0, nk, body, init, unroll=unroll)
    o = acc / l
    for hh in range(rep):
        o_ref[:, hh * HEAD_DIM:(hh + 1) * HEAD_DIM] = o[hh * tq:(hh + 1) * tq]


def _gqa(qkv, b, s):
    tq = TQ_GQA
    tk = min(TK_GQA, s)
    view = qkv.reshape(N_SEC, b, s, SEC)
    o = pl.pallas_call(
        functools.partial(_gqa_kernel, tk=tk, unroll=min(UNROLL_GQA, s // tk)),
        grid=(b, N_KV_B, s // tq),
        in_specs=[pl.BlockSpec((None, None, tq, SEC), lambda bi, g, qi: (SEC_QB + g, bi, qi, 0)),
                  pl.BlockSpec((None, None, s, HEAD_DIM), lambda bi, g, qi: (SEC_KVB, bi, 0, g)),
                  pl.BlockSpec((None, None, s, HEAD_DIM), lambda bi, g, qi: (SEC_KVB, bi, 0, N_KV_B + g))],
        out_specs=pl.BlockSpec((None, tq, SEC), lambda bi, g, qi: (bi, qi, g)),
        out_shape=jax.ShapeDtypeStruct((b, s, WIDTH_B), F32),
        scratch_shapes=[pltpu.VMEM((s, 2 * HEAD_DIM), BF16)],
        compiler_params=_cparams(3),
        name="gqa",
    )(view, view, view)
    return o.reshape(b * s, WIDTH_B)


def _out_proj_kernel(o1_ref, o2_ref, o3_ref, l1_ref, l2_ref, l3_ref, ob_ref, x_ref, ga_ref, gb_ref, w_ref, y_ref):
    lses = (l1_ref[...], l2_ref[...], l3_ref[...])
    m = jnp.maximum(jnp.maximum(lses[0], lses[1]), lses[2])
    es = [jnp.exp(v - m) for v in lses]
    tot = es[0] + es[1] + es[2]
    ws = [e / tot for e in es]
    heads = []
    for h in range(N_HEADS_A):
        hg, hl = divmod(h, 4)
        acc = None
        for w, o_ref in zip(ws, (o1_ref, o2_ref, o3_ref)):
            term = w[hg][:, hl:hl + 1] * o_ref[h]
            acc = term if acc is None else acc + term
        heads.append(acc)
    oa = jnp.concatenate(heads, axis=1)
    oa = oa * lax.rsqrt(jnp.mean(oa * oa, axis=-1, keepdims=True) + EPS) * ga_ref[...]
    ob = ob_ref[...]
    ob = ob * lax.rsqrt(jnp.mean(ob * ob, axis=-1, keepdims=True) + EPS) * gb_ref[...]
    y = (jnp.dot(oa.astype(BF16), w_ref[:WIDTH_A, :], preferred_element_type=F32)
         + jnp.dot(ob.astype(BF16), w_ref[WIDTH_A:, :], preferred_element_type=F32))
    y_ref[...] = x_ref[...] + y


def _out_proj(os_, lses, ob, x, on_a, on_b, w_out, s):
    t = x.shape[0]
    tm = TM_OUT
    spt = s // tm
    row = lambda width: pl.BlockSpec((tm, width), lambda i: (i, 0))
    o_spec = pl.BlockSpec((None, N_HEADS_A, tm, HEAD_DIM), lambda i: (i // spt, 0, i % spt, 0))
    lse_spec = pl.BlockSpec((2, None, tm, LANES), lambda i: (0, i // spt, i % spt, 0))
    const = lambda shape: pl.BlockSpec(shape, lambda i: (0, 0))
    return pl.pallas_call(
        _out_proj_kernel,
        grid=(t // tm,),
        in_specs=[o_spec, o_spec, o_spec, lse_spec, lse_spec, lse_spec, row(WIDTH_B),
                  row(D_MODEL), const((1, WIDTH_A)), const((1, WIDTH_B)), const((WIDTH_A + WIDTH_B, D_MODEL))],
        out_specs=row(D_MODEL),
        out_shape=jax.ShapeDtypeStruct((t, D_MODEL), F32),
        compiler_params=_cparams(1),
        name="out_proj",
    )(*os_, *lses, ob, x, on_a, on_b, w_out)


def _router_kernel(x_ref, g2_ref, w_ref, b_ref, h_ref, meta_ref, cnt_ref, run_ref):
    tm = x_ref.shape[0]

    @pl.when(pl.program_id(0) == 0)
    def _():
        run_ref[...] = jnp.zeros_like(run_ref)

    x = x_ref[...]
    h = x * lax.rsqrt(jnp.mean(x * x, axis=-1, keepdims=True) + EPS) * g2_ref[...]
    h_ref[...] = h
    logits = jnp.dot(h.astype(BF16), w_ref[...], preferred_element_type=F32) + b_ref[...]
    gl, el = logits[:, :LANES], logits[:, LANES:]
    lane = lax.broadcasted_iota(jnp.int32, (tm, LANES), 1).astype(F32)
    ninf = jnp.float32(-jnp.inf)
    big = jnp.float32(LANES)

    def first_argmax(v):
        top = jnp.max(v, axis=-1, keepdims=True)
        return top, jnp.min(jnp.where(v == top, lane, big), axis=-1, keepdims=True)

    gmask = lane < N_GROUPS
    gtop, gsel = first_argmax(jnp.where(gmask, gl, ninf))
    g_w = 1.0 / jnp.sum(jnp.where(gmask, jnp.exp(gl - gtop), 0.0), axis=-1, keepdims=True)
    lo = gsel * EXPERTS_PER_GROUP
    elm = jnp.where((lane >= lo) & (lane < lo + EXPERTS_PER_GROUP), el, ninf)
    t1, i1 = first_argmax(elm)
    t2, i2 = first_argmax(jnp.where(lane == i1, ninf, elm))
    e2 = jnp.exp(t2 - t1)
    gate1 = g_w * (1.0 / (1.0 + e2))
    gate2 = g_w * (e2 / (1.0 + e2))

    hot1, hot2 = lane == i1, lane == i2
    hot = (hot1 | hot2).astype(F32)
    r_i = lax.broadcasted_iota(jnp.int32, (tm, tm), 0)
    c_i = lax.broadcasted_iota(jnp.int32, (tm, tm), 1)
    before = (c_i < r_i).astype(BF16)
    rank = jnp.dot(before, hot.astype(BF16), preferred_element_type=F32) + run_ref[...]
    rank1 = jnp.sum(jnp.where(hot1, rank, 0.0), axis=-1, keepdims=True)
    rank2 = jnp.sum(jnp.where(hot2, rank, 0.0), axis=-1, keepdims=True)
    run = run_ref[...] + jnp.sum(hot, axis=0, keepdims=True)
    run_ref[...] = run
    cnt_ref[...] = run

    meta = jnp.zeros((tm, LANES), F32)
    for k, v in enumerate((i1, i2, rank1, rank2, gate1, gate2)):
        meta = jnp.where(lane == k, v, meta)
    meta_ref[...] = meta


def _router(x2, g2, w_r, b_r):
    t = x2.shape[0]
    tm = TM_ROUTE
    return pl.pallas_call(
        _router_kernel,
        grid=(t // tm,),
        in_specs=[pl.BlockSpec((tm, D_MODEL), lambda i: (i, 0)),
                  pl.BlockSpec((1, D_MODEL), lambda i: (0, 0)),
                  pl.BlockSpec((D_MODEL, 2 * LANES), lambda i: (0, 0)),
                  pl.BlockSpec((1, 2 * LANES), lambda i: (0, 0))],
        out_specs=[pl.BlockSpec((tm, D_MODEL), lambda i: (i, 0)),
                   pl.BlockSpec((tm, LANES), lambda i: (i, 0)),
                   pl.BlockSpec((1, LANES), lambda i: (0, 0))],
        out_shape=[jax.ShapeDtypeStruct((t, D_MODEL), F32),
                   jax.ShapeDtypeStruct((t, LANES), F32),
                   jax.ShapeDtypeStruct((1, LANES), F32)],
        scratch_shapes=[pltpu.VMEM((1, LANES), F32)],
        compiler_params=_cparams(1),
        name="router",
    )(x2, g2, w_r, b_r)


def _dispatch_kernel(plan_ref, dest_ref, h_ref, xs_ref, zero_ref, zsem, sem):
    i = pl.program_id(0)
    tt = dest_ref.shape[2] // TOP_K
    rows = zero_ref.shape[0]
    p_rows = xs_ref.shape[0]

    def pad_block(e):
        return plan_ref[N_EXPERTS + e] - rows, plan_ref[e] > 0

    def tail_block(c):
        row0 = plan_ref[2 * N_EXPERTS - 1] + c * rows
        return row0, row0 < p_rows

    def for_zero_blocks(block, action):
        def body(c, carry):
            row0, live = block(c)

            @pl.when(live)
            def _():
                dst = xs_ref.at[pl.ds(pl.multiple_of(row0, rows), rows)]
                action(pltpu.make_async_copy(zero_ref, dst, zsem))
            return carry

        lax.fori_loop(0, N_EXPERTS, body, 0)

    @pl.when(i == 0)
    def _():
        zero_ref[...] = jnp.zeros_like(zero_ref)
        for block in (pad_block, tail_block):
            for_zero_blocks(block, lambda cp: cp.start())
        for block in (pad_block, tail_block):
            for_zero_blocks(block, lambda cp: cp.wait())

    def row_copy(tok, k):
        dst = dest_ref[0, 0, TOP_K * tok + k]
        return pltpu.make_async_copy(h_ref.at[pl.ds(tok, 1)], xs_ref.at[pl.ds(dst, 1)], sem)

    def start_rows(tok, c):
        for k in range(TOP_K):
            row_copy(tok, k).start()
        return c

    def wait_rows(tok, c):
        for k in range(TOP_K):
            row_copy(tok, k).wait()
        return c

    lax.fori_loop(0, tt, start_rows, 0)
    lax.fori_loop(0, tt, wait_rows, 0)


def _dispatch(plan, dest, h2, p_rows):
    t = h2.shape[0]
    tt = TT_ROWS
    return pl.pallas_call(
        _dispatch_kernel,
        grid_spec=pltpu.PrefetchScalarGridSpec(
            num_scalar_prefetch=1,
            grid=(t // tt,),
            in_specs=[pl.BlockSpec((1, 1, TOP_K * tt), lambda i, plan: (i, 0, 0), memory_space=pltpu.SMEM),
                      pl.BlockSpec((tt, D_MODEL), lambda i, plan: (i, 0))],
            out_specs=pl.BlockSpec(memory_space=pl.ANY),
            scratch_shapes=[pltpu.VMEM((MOE_ROWS, D_MODEL), F32), pltpu.SemaphoreType.DMA(()),
                            pltpu.SemaphoreType.DMA(())]),
        out_shape=jax.ShapeDtypeStruct((p_rows, D_MODEL), F32),
        compiler_params=_cparams(1),
        name="dispatch",
    )(plan, dest, h2)


def _experts_kernel(blk_ref, nused_ref, xs_ref, wg_ref, wu_ref, wd_ref, ys_ref):
    i = pl.program_id(0)

    @pl.when(i < nused_ref[0])
    def _():
        x = xs_ref[...].astype(BF16)
        g = jnp.dot(x, wg_ref[...], preferred_element_type=F32)
        u = jnp.dot(x, wu_ref[...], preferred_element_type=F32)
        hid = g * (1.0 / (1.0 + jnp.exp(-g))) * u
        ys_ref[...] = jnp.dot(hid.astype(BF16), wd_ref[...], preferred_element_type=F32)

    @pl.when(i >= nused_ref[0])
    def _():
        ys_ref[...] = jnp.zeros_like(ys_ref)


def _experts(blk_exp, nused, xs, w_gate, w_up, w_down):
    p_rows = xs.shape[0]
    rows = MOE_ROWS
    live = lambda i, blk, nused: jnp.minimum(i, jnp.maximum(nused[0] - 1, 0))
    return pl.pallas_call(
        _experts_kernel,
        grid_spec=pltpu.PrefetchScalarGridSpec(
            num_scalar_prefetch=2,
            grid=(p_rows // rows,),
            in_specs=[pl.BlockSpec((rows, D_MODEL), lambda i, blk, nused: (live(i, blk, nused), 0)),
                      pl.BlockSpec((None, D_MODEL, D_EXPERT), lambda i, blk, nused: (blk[i], 0, 0)),
                      pl.BlockSpec((None, D_MODEL, D_EXPERT), lambda i, blk, nused: (blk[i], 0, 0)),
                      pl.BlockSpec((None, D_EXPERT, D_MODEL), lambda i, blk, nused: (blk[i], 0, 0))],
            out_specs=pl.BlockSpec((rows, D_MODEL), lambda i, blk, nused: (i, 0))),
        out_shape=jax.ShapeDtypeStruct((p_rows, D_MODEL), F32),
        compiler_params=_cparams(1),
        name="experts",
    )(blk_exp, nused, xs, w_gate, w_up, w_down)


def _combine_kernel(dest_ref, x_ref, meta_ref, ys_ref, o_ref, buf_ref, sem):
    tt = x_ref.shape[0]

    def row_copy(tok, k):
        src = dest_ref[0, 0, TOP_K * tok + k]
        return pltpu.make_async_copy(ys_ref.at[pl.ds(src, 1)], buf_ref.at[k, pl.ds(tok, 1)], sem)

    def start_rows(tok, c):
        for k in range(TOP_K):
            row_copy(tok, k).start()
        return c

    def wait_rows(tok, c):
        for k in range(TOP_K):
            row_copy(tok, k).wait()
        return c

    lax.fori_loop(0, tt, start_rows, 0)
    lax.fori_loop(0, tt, wait_rows, 0)
    meta = meta_ref[...]
    moe = meta[:, 4:5] * buf_ref[0] + meta[:, 5:6] * buf_ref[1]
    o_ref[...] = x_ref[...] + moe


def _combine(dest, x2, meta, ys):
    t = x2.shape[0]
    tt = TT_ROWS
    return pl.pallas_call(
        _combine_kernel,
        grid=(t // tt,),
        in_specs=[pl.BlockSpec((1, 1, TOP_K * tt), lambda i: (i, 0, 0), memory_space=pltpu.SMEM),
                  pl.BlockSpec((tt, D_MODEL), lambda i: (i, 0)),
                  pl.BlockSpec((tt, LANES), lambda i: (i, 0)),
                  pl.BlockSpec(memory_space=pl.ANY)],
        out_specs=pl.BlockSpec((tt, D_MODEL), lambda i: (i, 0)),
        out_shape=jax.ShapeDtypeStruct((t, D_MODEL), F32),
        scratch_shapes=[pltpu.VMEM((TOP_K, tt, D_MODEL), F32), pltpu.SemaphoreType.DMA(())],
        compiler_params=_cparams(1),
        name="combine",
    )(dest, x2, meta, ys)


def _rope_tables(s):
    rows = s // GRID_W
    r, c = jnp.meshgrid(jnp.arange(rows, dtype=F32), jnp.arange(GRID_W, dtype=F32), indexing='ij')
    n_freq = HEAD_DIM // 4
    inv = ROPE_THETA ** (-jnp.arange(n_freq, dtype=F32) / n_freq)
    ang_r = r.reshape(-1)[:, None] * inv
    ang_c = c.reshape(-1)[:, None] * inv
    ang = jnp.concatenate([ang_r, ang_r, ang_c, ang_c], axis=-1)
    sign = np.tile(np.repeat(np.array([-1.0, 1.0], np.float32), n_freq), 2)
    return jnp.cos(ang), jnp.sin(ang) * sign


def _moe_plan(meta, cnt, t):
    rows = MOE_ROWS
    p_rows = TOP_K * t + N_EXPERTS * rows
    nblk = p_rows // rows
    counts = cnt[0, :N_EXPERTS].astype(jnp.int32)
    pcounts = (counts + rows - 1) // rows * rows
    pends = jnp.cumsum(pcounts)
    pstarts = pends - pcounts
    expert = meta[:, 0:TOP_K].astype(jnp.int32)
    rank = meta[:, TOP_K:2 * TOP_K].astype(jnp.int32)
    ids = jnp.arange(N_EXPERTS, dtype=jnp.int32)
    start = jnp.sum(jnp.where(expert[..., None] == ids, pstarts, 0), axis=-1)
    dest = (start + rank).reshape(t // TT_ROWS, 1, TOP_K * TT_ROWS)
    nused = pends[-1] // rows
    row0 = jnp.arange(nblk, dtype=jnp.int32) * rows
    blk = jnp.sum(pends[None, :] <= jnp.minimum(row0, pends[-1] - rows)[:, None], axis=1).astype(jnp.int32)
    plan = jnp.concatenate([pcounts, pends]).astype(jnp.int32)
    return plan, dest, blk, nused.reshape(1).astype(jnp.int32), p_rows


def _trunk(x, p):
    b, s, _ = x.shape
    t = b * s
    assert s % (TQ_DIL * max(d for _, d in DILATED)) == 0 and s % TM_IN == 0 and t % TT_ROWS == 0
    x = x.reshape(t, D_MODEL)
    cos, sin = _rope_tables(s)
    qkv, r4, r16 = _in_proj(x, p['norm1_g'], p['w_in'], p['gain'], cos, sin, b, s)
    residue_major = {1: qkv.reshape(N_SEC, b, 1, s, SEC), 4: r4, 16: r16}
    branches = [_dilated_branch(residue_major[d], p['band'][i], b, s, d) for i, (_, d) in enumerate(DILATED)]
    ob = _gqa(qkv, b, s)
    x2 = _out_proj([o for o, _ in branches], [l for _, l in branches], ob, x, p['on_a'], p['on_b'], p['w_out'], s)
    h2, meta, cnt = _router(x2, p['norm2_g'], p['w_r'], p['b_r'])
    plan, dest, blk, nused, p_rows = _moe_plan(meta, cnt, t)
    xs = _dispatch(plan, dest, h2, p_rows)
    ys = _experts(blk, nused, xs, p['w_gate'], p['w_up'], p['w_down'])
    y = _combine(dest, x2, meta, ys)
    return y.reshape(b, s, D_MODEL)


def kernel(x_prompt, x_sample, norm1_g, w_in, qn_a, kn_a, qn_b, kn_b, rel_bias, on_a, on_b, w_out, norm2_g, rg_w,
           rg_b, re_w, re_b, w_gate, w_up, w_down):
    assert norm1_g.shape[0] == 1, "one layer"
    scale = HEAD_DIM ** -0.5
    ones_a = jnp.ones((WIDTH_A,), F32)
    gain = jnp.concatenate([jnp.tile(qn_a[0], N_HEADS_A) * scale, jnp.tile(kn_a[0], N_HEADS_A), ones_a,
                            jnp.tile(qn_b[0], N_HEADS_B) * (scale * LOG2E), jnp.tile(kn_b[0], N_KV_B),
                            jnp.ones((KV_WIDTH_B,), F32)]).reshape(1, IN_WIDTH)
    pad_g = jnp.zeros((D_MODEL, LANES - N_GROUPS), F32)
    pad_e = jnp.zeros((D_MODEL, LANES - N_EXPERTS), F32)
    params = dict(
        norm1_g=norm1_g, w_in=w_in[0].astype(BF16), gain=gain,
        band=[_band_tables(rel_bias, d) for _, d in DILATED],
        on_a=on_a, on_b=on_b, w_out=w_out[0].astype(BF16), norm2_g=norm2_g,
        w_r=jnp.concatenate([rg_w[0], pad_g, re_w[0], pad_e], axis=1).astype(BF16),
        b_r=jnp.concatenate([rg_b[0], jnp.zeros((LANES - N_GROUPS,), F32), re_b[0],
                             jnp.zeros((LANES - N_EXPERTS,), F32)]).reshape(1, 2 * LANES),
        w_gate=w_gate[0].astype(BF16), w_up=w_up[0].astype(BF16), w_down=w_down[0].astype(BF16))
    return _trunk(x_prompt, params), _trunk(x_sample, params)
```

```python
import functools

import numpy as np
import jax
import jax.numpy as jnp
from jax import lax
from jax.experimental import pallas as pl
from jax.experimental.pallas import tpu as pltpu

D_MODEL = 2048
HEAD_DIM = 128
N_HEADS_A = 8
N_HEADS_B = 8
N_KV_B = 2
WIDTH_A = N_HEADS_A * HEAD_DIM
WIDTH_B = N_HEADS_B * HEAD_DIM
KV_WIDTH_B = N_KV_B * HEAD_DIM
IN_WIDTH = 3 * WIDTH_A + WIDTH_B + 2 * KV_WIDTH_B
DILATED = ((128, 1), (512, 4), (2048, 16))
GRID_W = 64
ROPE_THETA = 10000.0
NUM_BUCKETS = 32
MAX_DISTANCE = 1024
N_GROUPS = 4
EXPERTS_PER_GROUP = 8
N_EXPERTS = N_GROUPS * EXPERTS_PER_GROUP
TOP_K = 2
D_EXPERT = D_MODEL // 2
EPS = 1e-6
NEG_INF = -1e30
LOG2E = float(np.log2(np.e))

LANES = 128
SEC = 4 * HEAD_DIM
N_SEC = IN_WIDTH // SEC
SEC_GROUP = 3
SEC_QA, SEC_KA, SEC_VA, SEC_QB, SEC_KVB = 0, 2, 4, 6, 8
BAND = 64
VMEM_LIMIT = 56 * 1024 * 1024

TM_IN = 512
TQ_DIL = 128
DIL_STEP = {1: (512, 1), 4: (256, 2), 16: (128, 4)}
TQ_GQA = 256
TK_GQA = 512
UNROLL_GQA = 16
TM_OUT = 256
TM_ROUTE = 256
TT_ROWS = 256
ROW_UNROLL = 8
MOE_ROWS = 256

F32 = jnp.float32
BF16 = jnp.bfloat16


def _cparams(n_axes):
    return pltpu.CompilerParams(dimension_semantics=("arbitrary",) * n_axes, vmem_limit_bytes=VMEM_LIMIT)


def _in_proj_kernel(x_ref, g1_ref, w_ref, gain_ref, cos_ref, sin_ref, o_ref, r4_ref, r16_ref, h_ref, y_ref, y4_ref):
    jp = pl.program_id(1)
    tm = x_ref.shape[0]
    heads = SEC // HEAD_DIM

    @pl.when(jp == 0)
    def _():
        x = x_ref[...]
        ms = jnp.mean(x * x, axis=-1, keepdims=True)
        h_ref[...] = (x * lax.rsqrt(ms + EPS) * g1_ref[...]).astype(BF16)

    lane = lax.broadcasted_iota(jnp.int32, (tm, HEAD_DIM), 1)
    first_quarter = (lane & 32) == 0

    def section(half, norm, rope, residue_major=False):
        p = jnp.dot(h_ref[...], w_ref[:, half * SEC:(half + 1) * SEC], preferred_element_type=F32)
        for hh in range(heads):
            sl = slice(hh * HEAD_DIM, (hh + 1) * HEAD_DIM)
            y = p[:, sl]
            if norm[hh]:
                ms = jnp.mean(y * y, axis=-1, keepdims=True)
                y = y * lax.rsqrt(ms + EPS)
            y = y * gain_ref[:, half * SEC + hh * HEAD_DIM:half * SEC + (hh + 1) * HEAD_DIM]
            if rope[hh]:
                partner = jnp.where(first_quarter, pltpu.roll(y, HEAD_DIM - 32, 1), pltpu.roll(y, 32, 1))
                y = y * cos_ref[...] + partner * sin_ref[...]
            o_ref[half, :, sl] = y.astype(BF16)
            if residue_major:
                slab = half * heads + hh
                y_ref[slab] = y
                for r in range(4):
                    y4 = y_ref[slab, pl.ds(r, tm // 4, stride=4), :]
                    r4_ref[half, r, :, sl] = y4.astype(BF16)
                    y4_ref[slab, pl.ds(r * (tm // 4), tm // 4), :] = y4
                for r in range(4):
                    for q in range(4):
                        y16 = y4_ref[slab, pl.ds(r * (tm // 4) + q, tm // 16, stride=4), :]
                        r16_ref[half, 4 * q + r, :, sl] = y16.astype(BF16)

    yes, no = (True,) * heads, (False,) * heads
    kv_b = (True,) * N_KV_B + (False,) * (heads - N_KV_B)

    def flags(sec):
        if sec < SEC_VA:
            return yes, no, True
        if sec < SEC_QB:
            return no, no, True
        if sec < SEC_KVB:
            return yes, yes, False
        return kv_b, kv_b, False

    for group in range(N_SEC // SEC_GROUP):
        @pl.when(jp == group)
        def _(group=group):
            for part in range(SEC_GROUP):
                section(part, *flags(group * SEC_GROUP + part))


def _in_proj(x, g1, w_in, gain, cos, sin, b, s):
    t = x.shape[0]
    tm = TM_IN
    spt = s // tm
    grp = SEC_GROUP
    n_dil = SEC_QB // grp

    def residue_major(d):
        spec = pl.BlockSpec((grp, None, d, tm // d, SEC),
                            lambda i, jp: (jnp.minimum(jp, n_dil - 1), i // spt, 0, i % spt, 0))
        return spec, jax.ShapeDtypeStruct((grp * n_dil, b, d, s // d, SEC), BF16)

    (r4_spec, r4_shape), (r16_spec, r16_shape) = residue_major(4), residue_major(16)
    return pl.pallas_call(
        _in_proj_kernel,
        grid=(t // tm, N_SEC // grp),
        in_specs=[
            pl.BlockSpec((tm, D_MODEL), lambda i, jp: (i, 0)),
            pl.BlockSpec((1, D_MODEL), lambda i, jp: (0, 0)),
            pl.BlockSpec((D_MODEL, grp * SEC), lambda i, jp: (0, jp)),
            pl.BlockSpec((1, grp * SEC), lambda i, jp: (0, jp)),
            pl.BlockSpec((tm, HEAD_DIM), lambda i, jp: (i % spt, 0)),
            pl.BlockSpec((tm, HEAD_DIM), lambda i, jp: (i % spt, 0)),
        ],
        out_specs=[pl.BlockSpec((grp, tm, SEC), lambda i, jp: (jp, i, 0)), r4_spec, r16_spec],
        out_shape=[jax.ShapeDtypeStruct((N_SEC, t, SEC), BF16), r4_shape, r16_shape],
        scratch_shapes=[pltpu.VMEM((tm, D_MODEL), BF16), pltpu.VMEM((grp * SEC // HEAD_DIM, tm, HEAD_DIM), F32),
                        pltpu.VMEM((grp * SEC // HEAD_DIM, tm, HEAD_DIM), F32)],
        compiler_params=_cparams(2),
        name="in_proj",
    )(x, g1, w_in, gain, cos, sin)


def _dilated_kernel(q_ref, kp_ref, kc_ref, kn_ref, vp_ref, vc_ref, vn_ref, a_ref, o_ref, lse_ref, *, nq, unroll):
    d, rows = q_ref.shape[0], q_ref.shape[1]
    tq = TQ_DIL
    nb = rows // tq
    lane = lax.broadcasted_iota(jnp.int32, (tq, LANES), 1)
    nt = (((1,), (1,)), ((), ()))
    first, last = pl.program_id(1) == 0, pl.program_id(1) == nq - 1

    def variant(j):
        lo, hi = (first if j == 0 else False), (last if j == nb - 1 else False)
        if lo is False and hi is False:
            return 1
        if hi is False:
            return jnp.where(lo, 0, 1)
        if lo is False:
            return jnp.where(hi, 2, 1)
        return jnp.where(lo & hi, 3, jnp.where(lo, 0, jnp.where(hi, 2, 1)))

    def residue(r, carry):
        heads = range(SEC // HEAD_DIM)
        sls = [slice(hh * HEAD_DIM, (hh + 1) * HEAD_DIM) for hh in heads]
        ks = [jnp.concatenate([kp_ref[r, :, sl], kc_ref[r, :, sl], kn_ref[r, :, sl]], axis=0) for sl in sls]
        vs = [jnp.concatenate([vp_ref[r, :, sl], vc_ref[r, :, sl], vn_ref[r, :, sl]], axis=0) for sl in sls]
        for j in range(nb):
            out_rows = pl.ds(j * tq * d + r, tq, stride=d) if d > 1 else pl.ds(j * tq, tq)
            keys = slice(j * tq, (j + 1) * tq + 2 * BAND)
            var = variant(j)
            lse_all = jnp.zeros((tq, LANES), F32)
            for hh in heads:
                q = q_ref[r, j * tq:(j + 1) * tq, sls[hh]]
                s = lax.dot_general(q, ks[hh][keys], nt, preferred_element_type=F32) + a_ref[var, hh]
                m = jnp.max(s, axis=-1, keepdims=True)
                p = jnp.exp(s - m)
                l = jnp.sum(p, axis=-1, keepdims=True)
                o = jnp.dot(p.astype(BF16), vs[hh][keys], preferred_element_type=F32)
                o_ref[hh, out_rows, :] = o / l
                lse_all = jnp.where(lane == hh, m + jnp.log(l), lse_all)
            lse_ref[out_rows, :] = lse_all
        return carry

    if d == 1:
        residue(0, 0)
    else:
        lax.fori_loop(0, d, residue, 0, unroll=unroll)


def _dilated_branch(rd, a_tab, b, s, d):
    ln = s // d
    tq = TQ_DIL
    rows, unroll = DIL_STEP[d]
    rows = min(rows, ln)
    nq = ln // rows
    edge = rows // BAND

    def cur(sec):
        return pl.BlockSpec((None, None, d, rows, SEC), lambda bi, qi, hg: (sec + hg, bi, 0, qi, 0))

    def prev(sec):
        return pl.BlockSpec((None, None, d, BAND, SEC),
                            lambda bi, qi, hg: (sec + hg, bi, 0, jnp.maximum(edge * qi - 1, 0), 0))

    def nxt(sec):
        return pl.BlockSpec((None, None, d, BAND, SEC),
                            lambda bi, qi, hg: (sec + hg, bi, 0, jnp.minimum(edge * (qi + 1), edge * nq - 1), 0))

    return pl.pallas_call(
        functools.partial(_dilated_kernel, nq=nq, unroll=unroll),
        grid=(b, nq, 2),
        in_specs=[cur(SEC_QA), prev(SEC_KA), cur(SEC_KA), nxt(SEC_KA), prev(SEC_VA), cur(SEC_VA), nxt(SEC_VA),
                  pl.BlockSpec((4, None, 4, tq, tq + 2 * BAND), lambda bi, qi, hg: (0, hg, 0, 0, 0))],
        out_specs=[pl.BlockSpec((None, 4, d * rows, HEAD_DIM), lambda bi, qi, hg: (bi, hg, qi, 0)),
                   pl.BlockSpec((None, None, d * rows, LANES), lambda bi, qi, hg: (hg, bi, qi, 0))],
        out_shape=[jax.ShapeDtypeStruct((b, N_HEADS_A, s, HEAD_DIM), F32),
                   jax.ShapeDtypeStruct((2, b, s, LANES), F32)],
        compiler_params=_cparams(3),
        name=f"dilated_d{d}",
    )(rd, rd, rd, rd, rd, rd, rd, a_tab)


def _t5_buckets(rel):
    nb = NUM_BUCKETS // 2
    max_exact = nb // 2
    n = np.abs(rel)
    large = max_exact + (np.log(np.maximum(n, 1) / max_exact) / np.log(MAX_DISTANCE / max_exact)
                         * (nb - max_exact)).astype(np.int32)
    large = np.minimum(large, nb - 1)
    return (rel > 0).astype(np.int32) * nb + np.where(n < max_exact, n, large).astype(np.int32)


def _band_tables(rel_bias, d):
    tq = TQ_DIL
    width = tq + 2 * BAND
    rel = np.arange(width + tq - 1) - (tq - 1) - BAND
    bucket = _t5_buckets(d * np.clip(rel, -BAND, BAND))
    diag = jnp.where(np.abs(rel) <= BAND, rel_bias.astype(F32)[bucket].T, NEG_INF)
    n = diag.shape[1]
    flat = jnp.tile(diag, (1, tq))[:, tq - 1:tq - 1 + tq * (n - 1)]
    table = flat.reshape(N_HEADS_A, tq, n - 1)[:, :, :width]
    c = np.arange(width)[None, None, :]
    prev_ok, next_ok = c >= BAND, c < BAND + tq
    tabs = [jnp.where(keep, table, NEG_INF) for keep in (prev_ok, c >= 0, next_ok, prev_ok & next_ok)]
    return jnp.stack(tabs).reshape(4, 2, 4, tq, width)


def _gqa_kernel(q_ref, k_ref, v_ref, o_ref, vx_ref, *, tk, unroll):
    tq = q_ref.shape[0]
    nk = k_ref.shape[0] // tk
    rep = SEC // HEAD_DIM

    @pl.when(pl.program_id(2) == 0)
    def _():
        vx_ref[:, :HEAD_DIM] = v_ref[...]
        vx_ref[:, HEAD_DIM:] = jnp.ones((vx_ref.shape[0], HEAD_DIM), BF16)

    q = jnp.concatenate([q_ref[:, hh * HEAD_DIM:(hh + 1) * HEAD_DIM] for hh in range(rep)], axis=0)
    nt = (((1,), (1,)), ((), ()))

    def body(c, carry):
        m, l, acc = carry
        start = pl.multiple_of(c * tk, tk)
        s = lax.dot_general(q, k_ref[pl.ds(start, tk), :], nt, preferred_element_type=F32)
        m_new = jnp.maximum(m, jnp.max(s, axis=-1, keepdims=True))
        alpha = jnp.exp2(m - m_new)
        p = jnp.exp2((s - m_new).astype(BF16))
        pv = jnp.dot(p, vx_ref[pl.ds(start, tk), :], preferred_element_type=F32)
        l = alpha * l + pv[:, HEAD_DIM:HEAD_DIM + 1]
        acc = alpha * acc + pv[:, :HEAD_DIM]
        return m_new, l, acc

    init = (jnp.full((rep * tq, 1), NEG_INF, F32), jnp.zeros((rep * tq, 1), F32),
            jnp.zeros((rep * tq, HEAD_DIM), F32))
    _, l, acc = lax.fori_loop(0, nk, body, init, unroll=unroll)
    o = acc / l
    for hh in range(rep):
        o_ref[:, hh * HEAD_DIM:(hh + 1) * HEAD_DIM] = o[hh * tq:(hh + 1) * tq]


def _gqa(qkv, b, s):
    tq = TQ_GQA
    tk = min(TK_GQA, s)
    view = qkv.reshape(N_SEC, b, s, SEC)
    o = pl.pallas_call(
        functools.partial(_gqa_kernel, tk=tk, unroll=min(UNROLL_GQA, s // tk)),
        grid=(b, N_KV_B, s // tq),
        in_specs=[pl.BlockSpec((None, None, tq, SEC), lambda bi, g, qi: (SEC_QB + g, bi, qi, 0)),
                  pl.BlockSpec((None, None, s, HEAD_DIM), lambda bi, g, qi: (SEC_KVB, bi, 0, g)),
                  pl.BlockSpec((None, None, s, HEAD_DIM), lambda bi, g, qi: (SEC_KVB, bi, 0, N_KV_B + g))],
        out_specs=pl.BlockSpec((None, tq, SEC), lambda bi, g, qi: (bi, qi, g)),
        out_shape=jax.ShapeDtypeStruct((b, s, WIDTH_B), F32),
        scratch_shapes=[pltpu.VMEM((s, 2 * HEAD_DIM), BF16)],
        compiler_params=_cparams(3),
        name="gqa",
    )(view, view, view)
    return o.reshape(b * s, WIDTH_B)


def _out_proj_kernel(o1_ref, o2_ref, o3_ref, l1_ref, l2_ref, l3_ref, ob_ref, x_ref, ga_ref, gb_ref, w_ref, y_ref):
    lses = (l1_ref[...], l2_ref[...], l3_ref[...])
    m = jnp.maximum(jnp.maximum(lses[0], lses[1]), lses[2])
    es = [jnp.exp(v - m) for v in lses]
    tot = es[0] + es[1] + es[2]
    ws = [e / tot for e in es]
    heads = []
    for h in range(N_HEADS_A):
        hg, hl = divmod(h, 4)
        acc = None
        for w, o_ref in zip(ws, (o1_ref, o2_ref, o3_ref)):
            term = w[hg][:, hl:hl + 1] * o_ref[h]
            acc = term if acc is None else acc + term
        heads.append(acc)
    oa = jnp.concatenate(heads, axis=1)
    oa = oa * lax.rsqrt(jnp.mean(oa * oa, axis=-1, keepdims=True) + EPS) * ga_ref[...]
    ob = ob_ref[...]
    ob = ob * lax.rsqrt(jnp.mean(ob * ob, axis=-1, keepdims=True) + EPS) * gb_ref[...]
    y = (jnp.dot(oa.astype(BF16), w_ref[:WIDTH_A, :], preferred_element_type=F32)
         + jnp.dot(ob.astype(BF16), w_ref[WIDTH_A:, :], preferred_element_type=F32))
    y_ref[...] = x_ref[...] + y


def _out_proj(os_, lses, ob, x, on_a, on_b, w_out, s):
    t = x.shape[0]
    tm = TM_OUT
    spt = s // tm
    row = lambda width: pl.BlockSpec((tm, width), lambda i: (i, 0))
    o_spec = pl.BlockSpec((None, N_HEADS_A, tm, HEAD_DIM), lambda i: (i // spt, 0, i % spt, 0))
    lse_spec = pl.BlockSpec((2, None, tm, LANES), lambda i: (0, i // spt, i % spt, 0))
    const = lambda shape: pl.BlockSpec(shape, lambda i: (0, 0))
    return pl.pallas_call(
        _out_proj_kernel,
        grid=(t // tm,),
        in_specs=[o_spec, o_spec, o_spec, lse_spec, lse_spec, lse_spec, row(WIDTH_B),
                  row(D_MODEL), const((1, WIDTH_A)), const((1, WIDTH_B)), const((WIDTH_A + WIDTH_B, D_MODEL))],
        out_specs=row(D_MODEL),
        out_shape=jax.ShapeDtypeStruct((t, D_MODEL), F32),
        compiler_params=_cparams(1),
        name="out_proj",
    )(*os_, *lses, ob, x, on_a, on_b, w_out)


def _router_kernel(x_ref, g2_ref, w_ref, b_ref, h_ref, meta_ref, cnt_ref, run_ref):
    tm = x_ref.shape[0]

    @pl.when(pl.program_id(0) == 0)
    def _():
        run_ref[...] = jnp.zeros_like(run_ref)

    x = x_ref[...]
    h = x * lax.rsqrt(jnp.mean(x * x, axis=-1, keepdims=True) + EPS) * g2_ref[...]
    h_ref[...] = h
    logits = jnp.dot(h.astype(BF16), w_ref[...], preferred_element_type=F32) + b_ref[...]
    gl, el = logits[:, :LANES], logits[:, LANES:]
    lane = lax.broadcasted_iota(jnp.int32, (tm, LANES), 1).astype(F32)
    ninf = jnp.float32(-jnp.inf)
    big = jnp.float32(LANES)

    def first_argmax(v):
        top = jnp.max(v, axis=-1, keepdims=True)
        return top, jnp.min(jnp.where(v == top, lane, big), axis=-1, keepdims=True)

    gmask = lane < N_GROUPS
    gtop, gsel = first_argmax(jnp.where(gmask, gl, ninf))
    g_w = 1.0 / jnp.sum(jnp.where(gmask, jnp.exp(gl - gtop), 0.0), axis=-1, keepdims=True)
    lo = gsel * EXPERTS_PER_GROUP
    elm = jnp.where((lane >= lo) & (lane < lo + EXPERTS_PER_GROUP), el, ninf)
    t1, i1 = first_argmax(elm)
    t2, i2 = first_argmax(jnp.where(lane == i1, ninf, elm))
    e2 = jnp.exp(t2 - t1)
    gate1 = g_w * (1.0 / (1.0 + e2))
    gate2 = g_w * (e2 / (1.0 + e2))

    hot1, hot2 = lane == i1, lane == i2
    hot = (hot1 | hot2).astype(F32)
    r_i = lax.broadcasted_iota(jnp.int32, (tm, tm), 0)
    c_i = lax.broadcasted_iota(jnp.int32, (tm, tm), 1)
    before = (c_i < r_i).astype(BF16)
    rank = jnp.dot(before, hot.astype(BF16), preferred_element_type=F32) + run_ref[...]
    rank1 = jnp.sum(jnp.where(hot1, rank, 0.0), axis=-1, keepdims=True)
    rank2 = jnp.sum(jnp.where(hot2, rank, 0.0), axis=-1, keepdims=True)
    run = run_ref[...] + jnp.sum(hot, axis=0, keepdims=True)
    run_ref[...] = run
    cnt_ref[...] = run

    meta = jnp.zeros((tm, LANES), F32)
    for k, v in enumerate((i1, i2, rank1, rank2, gate1, gate2)):
        meta = jnp.where(lane == k, v, meta)
    meta_ref[...] = meta


def _router(x2, g2, w_r, b_r):
    t = x2.shape[0]
    tm = TM_ROUTE
    return pl.pallas_call(
        _router_kernel,
        grid=(t // tm,),
        in_specs=[pl.BlockSpec((tm, D_MODEL), lambda i: (i, 0)),
                  pl.BlockSpec((1, D_MODEL), lambda i: (0, 0)),
                  pl.BlockSpec((D_MODEL, 2 * LANES), lambda i: (0, 0)),
                  pl.BlockSpec((1, 2 * LANES), lambda i: (0, 0))],
        out_specs=[pl.BlockSpec((tm, D_MODEL), lambda i: (i, 0)),
                   pl.BlockSpec((tm, LANES), lambda i: (i, 0)),
                   pl.BlockSpec((1, LANES), lambda i: (0, 0))],
        out_shape=[jax.ShapeDtypeStruct((t, D_MODEL), F32),
                   jax.ShapeDtypeStruct((t, LANES), F32),
                   jax.ShapeDtypeStruct((1, LANES), F32)],
        scratch_shapes=[pltpu.VMEM((1, LANES), F32)],
        compiler_params=_cparams(1),
        name="router",
    )(x2, g2, w_r, b_r)


def _dispatch_kernel(plan_ref, dest_ref, h_ref, xs_ref, zero_ref, zsem, sem):
    i = pl.program_id(0)
    tt = dest_ref.shape[2] // TOP_K
    rows = zero_ref.shape[0]
    p_rows = xs_ref.shape[0]

    def pad_block(e):
        return plan_ref[N_EXPERTS + e] - rows, plan_ref[e] > 0

    def tail_block(c):
        row0 = plan_ref[2 * N_EXPERTS - 1] + c * rows
        return row0, row0 < p_rows

    def for_zero_blocks(block, action):
        def body(c, carry):
            row0, live = block(c)

            @pl.when(live)
            def _():
                dst = xs_ref.at[pl.ds(pl.multiple_of(row0, rows), rows)]
                action(pltpu.make_async_copy(zero_ref, dst, zsem))
            return carry

        lax.fori_loop(0, N_EXPERTS, body, 0)

    @pl.when(i == 0)
    def _():
        zero_ref[...] = jnp.zeros_like(zero_ref)
        for block in (pad_block, tail_block):
            for_zero_blocks(block, lambda cp: cp.start())
        for block in (pad_block, tail_block):
            for_zero_blocks(block, lambda cp: cp.wait())

    def row_copy(tok, k):
        dst = dest_ref[0, 0, TOP_K * tok + k]
        return pltpu.make_async_copy(h_ref.at[pl.ds(tok, 1)], xs_ref.at[pl.ds(dst, 1)], sem)

    def start_rows(tok, c):
        for k in range(TOP_K):
            row_copy(tok, k).start()
        return c

    def wait_rows(tok, c):
        for k in range(TOP_K):
            row_copy(tok, k).wait()
        return c

    lax.fori_loop(0, tt, start_rows, 0, unroll=ROW_UNROLL)
    lax.fori_loop(0, tt, wait_rows, 0, unroll=ROW_UNROLL)


def _dispatch(plan, dest, h2, p_rows):
    t = h2.shape[0]
    tt = TT_ROWS
    return pl.pallas_call(
        _dispatch_kernel,
        grid_spec=pltpu.PrefetchScalarGridSpec(
            num_scalar_prefetch=1,
            grid=(t // tt,),
            in_specs=[pl.BlockSpec((1, 1, TOP_K * tt), lambda i, plan: (i, 0, 0), memory_space=pltpu.SMEM),
                      pl.BlockSpec((tt, D_MODEL), lambda i, plan: (i, 0))],
            out_specs=pl.BlockSpec(memory_space=pl.ANY),
            scratch_shapes=[pltpu.VMEM((MOE_ROWS, D_MODEL), F32), pltpu.SemaphoreType.DMA(()),
                            pltpu.SemaphoreType.DMA(())]),
        out_shape=jax.ShapeDtypeStruct((p_rows, D_MODEL), F32),
        compiler_params=_cparams(1),
        name="dispatch",
    )(plan, dest, h2)


def _experts_kernel(blk_ref, nused_ref, xs_ref, wg_ref, wu_ref, wd_ref, ys_ref):
    i = pl.program_id(0)

    @pl.when(i < nused_ref[0])
    def _():
        x = xs_ref[...].astype(BF16)
        g = jnp.dot(x, wg_ref[...], preferred_element_type=F32)
        u = jnp.dot(x, wu_ref[...], preferred_element_type=F32)
        hid = g * (1.0 / (1.0 + jnp.exp(-g))) * u
        ys_ref[...] = jnp.dot(hid.astype(BF16), wd_ref[...], preferred_element_type=F32)

    @pl.when(i >= nused_ref[0])
    def _():
        ys_ref[...] = jnp.zeros_like(ys_ref)


def _experts(blk_exp, nused, xs, w_gate, w_up, w_down):
    p_rows = xs.shape[0]
    rows = MOE_ROWS
    live = lambda i, blk, nused: jnp.minimum(i, jnp.maximum(nused[0] - 1, 0))
    return pl.pallas_call(
        _experts_kernel,
        grid_spec=pltpu.PrefetchScalarGridSpec(
            num_scalar_prefetch=2,
            grid=(p_rows // rows,),
            in_specs=[pl.BlockSpec((rows, D_MODEL), lambda i, blk, nused: (live(i, blk, nused), 0)),
                      pl.BlockSpec((None, D_MODEL, D_EXPERT), lambda i, blk, nused: (blk[i], 0, 0)),
                      pl.BlockSpec((None, D_MODEL, D_EXPERT), lambda i, blk, nused: (blk[i], 0, 0)),
                      pl.BlockSpec((None, D_EXPERT, D_MODEL), lambda i, blk, nused: (blk[i], 0, 0))],
            out_specs=pl.BlockSpec((rows, D_MODEL), lambda i, blk, nused: (i, 0))),
        out_shape=jax.ShapeDtypeStruct((p_rows, D_MODEL), F32),
        compiler_params=_cparams(1),
        name="experts",
    )(blk_exp, nused, xs, w_gate, w_up, w_down)


def _combine_kernel(dest_ref, x_ref, meta_ref, ys_ref, o_ref, buf_ref, sem):
    tt = x_ref.shape[0]

    def row_copy(tok, k):
        src = dest_ref[0, 0, TOP_K * tok + k]
        return pltpu.make_async_copy(ys_ref.at[pl.ds(src, 1)], buf_ref.at[k, pl.ds(tok, 1)], sem)

    def start_rows(tok, c):
        for k in range(TOP_K):
            row_copy(tok, k).start()
        return c

    def wait_rows(tok, c):
        for k in range(TOP_K):
            row_copy(tok, k).wait()
        return c

    lax.fori_loop(0, tt, start_rows, 0, unroll=ROW_UNROLL)
    lax.fori_loop(0, tt, wait_rows, 0, unroll=ROW_UNROLL)
    meta = meta_ref[...]
    moe = meta[:, 4:5] * buf_ref[0] + meta[:, 5:6] * buf_ref[1]
    o_ref[...] = x_ref[...] + moe


def _combine(dest, x2, meta, ys):
    t = x2.shape[0]
    tt = TT_ROWS
    return pl.pallas_call(
        _combine_kernel,
        grid=(t // tt,),
        in_specs=[pl.BlockSpec((1, 1, TOP_K * tt), lambda i: (i, 0, 0), memory_space=pltpu.SMEM),
                  pl.BlockSpec((tt, D_MODEL), lambda i: (i, 0)),
                  pl.BlockSpec((tt, LANES), lambda i: (i, 0)),
                  pl.BlockSpec(memory_space=pl.ANY)],
        out_specs=pl.BlockSpec((tt, D_MODEL), lambda i: (i, 0)),
        out_shape=jax.ShapeDtypeStruct((t, D_MODEL), F32),
        scratch_shapes=[pltpu.VMEM((TOP_K, tt, D_MODEL), F32), pltpu.SemaphoreType.DMA(())],
        compiler_params=_cparams(1),
        name="combine",
    )(dest, x2, meta, ys)


def _rope_tables(s):
    rows = s // GRID_W
    r, c = jnp.meshgrid(jnp.arange(rows, dtype=F32), jnp.arange(GRID_W, dtype=F32), indexing='ij')
    n_freq = HEAD_DIM // 4
    inv = ROPE_THETA ** (-jnp.arange(n_freq, dtype=F32) / n_freq)
    ang_r = r.reshape(-1)[:, None] * inv
    ang_c = c.reshape(-1)[:, None] * inv
    ang = jnp.concatenate([ang_r, ang_r, ang_c, ang_c], axis=-1)
    sign = np.tile(np.repeat(np.array([-1.0, 1.0], np.float32), n_freq), 2)
    return jnp.cos(ang), jnp.sin(ang) * sign


def _moe_plan(meta, cnt, t):
    rows = MOE_ROWS
    p_rows = TOP_K * t + N_EXPERTS * rows
    nblk = p_rows // rows
    counts = cnt[0, :N_EXPERTS].astype(jnp.int32)
    pcounts = (counts + rows - 1) // rows * rows
    pends = jnp.cumsum(pcounts)
    pstarts = pends - pcounts
    expert = meta[:, 0:TOP_K].astype(jnp.int32)
    rank = meta[:, TOP_K:2 * TOP_K].astype(jnp.int32)
    ids = jnp.arange(N_EXPERTS, dtype=jnp.int32)
    start = jnp.sum(jnp.where(expert[..., None] == ids, pstarts, 0), axis=-1)
    dest = (start + rank).reshape(t // TT_ROWS, 1, TOP_K * TT_ROWS)
    nused = pends[-1] // rows
    row0 = jnp.arange(nblk, dtype=jnp.int32) * rows
    blk = jnp.sum(pends[None, :] <= jnp.minimum(row0, pends[-1] - rows)[:, None], axis=1).astype(jnp.int32)
    plan = jnp.concatenate([pcounts, pends]).astype(jnp.int32)
    return plan, dest, blk, nused.reshape(1).astype(jnp.int32), p_rows


def _trunk(x, p):
    b, s, _ = x.shape
    t = b * s
    assert s % (TQ_DIL * max(d for _, d in DILATED)) == 0 and s % TM_IN == 0 and t % TT_ROWS == 0
    x = x.reshape(t, D_MODEL)
    cos, sin = _rope_tables(s)
    qkv, r4, r16 = _in_proj(x, p['norm1_g'], p['w_in'], p['gain'], cos, sin, b, s)
    residue_major = {1: qkv.reshape(N_SEC, b, 1, s, SEC), 4: r4, 16: r16}
    branches = [_dilated_branch(residue_major[d], p['band'][i], b, s, d) for i, (_, d) in enumerate(DILATED)]
    ob = _gqa(qkv, b, s)
    x2 = _out_proj([o for o, _ in branches], [l for _, l in branches], ob, x, p['on_a'], p['on_b'], p['w_out'], s)
    h2, meta, cnt = _router(x2, p['norm2_g'], p['w_r'], p['b_r'])
    plan, dest, blk, nused, p_rows = _moe_plan(meta, cnt, t)
    xs = _dispatch(plan, dest, h2, p_rows)
    ys = _experts(blk, nused, xs, p['w_gate'], p['w_up'], p['w_down'])
    y = _combine(dest, x2, meta, ys)
    return y.reshape(b, s, D_MODEL)


def kernel(x_prompt, x_sample, norm1_g, w_in, qn_a, kn_a, qn_b, kn_b, rel_bias, on_a, on_b, w_out, norm2_g, rg_w,
           rg_b, re_w, re_b, w_gate, w_up, w_down):
    assert norm1_g.shape[0] == 1, "one layer"
    scale = HEAD_DIM ** -0.5
    ones_a = jnp.ones((WIDTH_A,), F32)
    gain = jnp.concatenate([jnp.tile(qn_a[0], N_HEADS_A) * scale, jnp.tile(kn_a[0], N_HEADS_A), ones_a,
                            jnp.tile(qn_b[0], N_HEADS_B) * (scale * LOG2E), jnp.tile(kn_b[0], N_KV_B),
                            jnp.ones((KV_WIDTH_B,), F32)]).reshape(1, IN_WIDTH)
    pad_g = jnp.zeros((D_MODEL, LANES - N_GROUPS), F32)
    pad_e = jnp.zeros((D_MODEL, LANES - N_EXPERTS), F32)
    params = dict(
        norm1_g=norm1_g, w_in=w_in[0].astype(BF16), gain=gain,
        band=[_band_tables(rel_bias, d) for _, d in DILATED],
        on_a=on_a, on_b=on_b, w_out=w_out[0].astype(BF16), norm2_g=norm2_g,
        w_r=jnp.concatenate([rg_w[0], pad_g, re_w[0], pad_e], axis=1).astype(BF16),
        b_r=jnp.concatenate([rg_b[0], jnp.zeros((LANES - N_GROUPS,), F32), re_b[0],
                             jnp.zeros((LANES - N_EXPERTS,), F32)]).reshape(1, 2 * LANES),
        w_gate=w_gate[0].astype(BF16), w_up=w_up[0].astype(BF16), w_down=w_down[0].astype(BF16))
    return _trunk(x_prompt, params), _trunk(x_sample, params)
```

```python
import functools

import numpy as np
import jax
import jax.numpy as jnp
from jax import lax
from jax.experimental import pallas as pl
from jax.experimental.pallas import tpu as pltpu

D_MODEL = 2048
HEAD_DIM = 128
N_HEADS_A = 8
N_HEADS_B = 8
N_KV_B = 2
WIDTH_A = N_HEADS_A * HEAD_DIM
WIDTH_B = N_HEADS_B * HEAD_DIM
KV_WIDTH_B = N_KV_B * HEAD_DIM
IN_WIDTH = 3 * WIDTH_A + WIDTH_B + 2 * KV_WIDTH_B
DILATED = ((128, 1), (512, 4), (2048, 16))
GRID_W = 64
ROPE_THETA = 10000.0
NUM_BUCKETS = 32
MAX_DISTANCE = 1024
N_GROUPS = 4
EXPERTS_PER_GROUP = 8
N_EXPERTS = N_GROUPS * EXPERTS_PER_GROUP
TOP_K = 2
D_EXPERT = D_MODEL // 2
D_PACK = D_MODEL // 2
EPS = 1e-6
NEG_INF = -1e30
LOG2E = float(np.log2(np.e))

LANES = 128
SEC = 4 * HEAD_DIM
N_SEC = IN_WIDTH // SEC
SEC_GROUP = 3
SEC_QA, SEC_KA, SEC_VA, SEC_QB, SEC_KVB = 0, 2, 4, 6, 8
BAND = 64
VMEM_LIMIT = 56 * 1024 * 1024

TM_IN = 512
TQ_DIL = 128
DIL_STEP = {1: (1024, 1), 4: (256, 4), 16: (128, 8)}
TQ_GQA = 256
TK_GQA = 512
UNROLL_GQA = 16
TM_OUT = 256
TT_ROWS = 256
ROW_UNROLL = 8
MOE_ROWS = 256

F32 = jnp.float32
BF16 = jnp.bfloat16


def _cparams(n_axes):
    return pltpu.CompilerParams(dimension_semantics=("arbitrary",) * n_axes, vmem_limit_bytes=VMEM_LIMIT)


def _pack_halves(a):
    n = a.shape[1] // 2
    lo = pltpu.bitcast(a[:, :n].astype(BF16).astype(F32), jnp.uint32)
    hi = pltpu.bitcast(a[:, n:].astype(BF16).astype(F32), jnp.uint32)
    return (lo >> 16) | (hi & jnp.uint32(0xFFFF0000))


def _unpack_halves(u):
    return pltpu.bitcast(u << 16, F32), pltpu.bitcast(u & jnp.uint32(0xFFFF0000), F32)


def _in_proj_kernel(x_ref, g1_ref, w_ref, gain_ref, cos_ref, sin_ref, o_ref, r4_ref, r16_ref, h_ref, y_ref, y4_ref):
    jp = pl.program_id(1)
    tm = x_ref.shape[0]
    heads = SEC // HEAD_DIM

    @pl.when(jp == 0)
    def _():
        x = x_ref[...]
        ms = jnp.mean(x * x, axis=-1, keepdims=True)
        h_ref[...] = (x * lax.rsqrt(ms + EPS) * g1_ref[...]).astype(BF16)

    lane = lax.broadcasted_iota(jnp.int32, (tm, HEAD_DIM), 1)
    first_quarter = (lane & 32) == 0

    def section(half, norm, rope, residue_major=False):
        p = jnp.dot(h_ref[...], w_ref[:, half * SEC:(half + 1) * SEC], preferred_element_type=F32)
        for hh in range(heads):
            sl = slice(hh * HEAD_DIM, (hh + 1) * HEAD_DIM)
            y = p[:, sl]
            if norm[hh]:
                ms = jnp.mean(y * y, axis=-1, keepdims=True)
                y = y * lax.rsqrt(ms + EPS)
            y = y * gain_ref[:, half * SEC + hh * HEAD_DIM:half * SEC + (hh + 1) * HEAD_DIM]
            if rope[hh]:
                partner = jnp.where(first_quarter, pltpu.roll(y, HEAD_DIM - 32, 1), pltpu.roll(y, 32, 1))
                y = y * cos_ref[...] + partner * sin_ref[...]
            o_ref[half, :, sl] = y.astype(BF16)
            if residue_major:
                slab = half * heads + hh
                y_ref[slab] = y
                for r in range(4):
                    y4 = y_ref[slab, pl.ds(r, tm // 4, stride=4), :]
                    r4_ref[half, r, :, sl] = y4.astype(BF16)
                    y4_ref[slab, pl.ds(r * (tm // 4), tm // 4), :] = y4
                for r in range(4):
                    for q in range(4):
                        y16 = y4_ref[slab, pl.ds(r * (tm // 4) + q, tm // 16, stride=4), :]
                        r16_ref[half, 4 * q + r, :, sl] = y16.astype(BF16)

    yes, no = (True,) * heads, (False,) * heads
    kv_b = (True,) * N_KV_B + (False,) * (heads - N_KV_B)

    def flags(sec):
        if sec < SEC_VA:
            return yes, no, True
        if sec < SEC_QB:
            return no, no, True
        if sec < SEC_KVB:
            return yes, yes, False
        return kv_b, kv_b, False

    for group in range(N_SEC // SEC_GROUP):
        @pl.when(jp == group)
        def _(group=group):
            for part in range(SEC_GROUP):
                section(part, *flags(group * SEC_GROUP + part))


def _in_proj(x, g1, w_in, gain, cos, sin, b, s):
    t = x.shape[0]
    tm = TM_IN
    spt = s // tm
    grp = SEC_GROUP
    n_dil = SEC_QB // grp

    def residue_major(d):
        spec = pl.BlockSpec((grp, None, d, tm // d, SEC),
                            lambda i, jp: (jnp.minimum(jp, n_dil - 1), i // spt, 0, i % spt, 0))
        return spec, jax.ShapeDtypeStruct((grp * n_dil, b, d, s // d, SEC), BF16)

    (r4_spec, r4_shape), (r16_spec, r16_shape) = residue_major(4), residue_major(16)
    return pl.pallas_call(
        _in_proj_kernel,
        grid=(t // tm, N_SEC // grp),
        in_specs=[
            pl.BlockSpec((tm, D_MODEL), lambda i, jp: (i, 0)),
            pl.BlockSpec((1, D_MODEL), lambda i, jp: (0, 0)),
            pl.BlockSpec((D_MODEL, grp * SEC), lambda i, jp: (0, jp)),
            pl.BlockSpec((1, grp * SEC), lambda i, jp: (0, jp)),
            pl.BlockSpec((tm, HEAD_DIM), lambda i, jp: (i % spt, 0)),
            pl.BlockSpec((tm, HEAD_DIM), lambda i, jp: (i % spt, 0)),
        ],
        out_specs=[pl.BlockSpec((grp, tm, SEC), lambda i, jp: (jp, i, 0)), r4_spec, r16_spec],
        out_shape=[jax.ShapeDtypeStruct((N_SEC, t, SEC), BF16), r4_shape, r16_shape],
        scratch_shapes=[pltpu.VMEM((tm, D_MODEL), BF16), pltpu.VMEM((grp * SEC // HEAD_DIM, tm, HEAD_DIM), F32),
                        pltpu.VMEM((grp * SEC // HEAD_DIM, tm, HEAD_DIM), F32)],
        compiler_params=_cparams(2),
        name="in_proj",
    )(x, g1, w_in, gain, cos, sin)


def _dilated_kernel(q_ref, kp_ref, kc_ref, kn_ref, vp_ref, vc_ref, vn_ref, a_ref, o_ref, lse_ref, *, nq, unroll):
    d, rows = q_ref.shape[0], q_ref.shape[1]
    tq = TQ_DIL
    nb = rows // tq
    lane = lax.broadcasted_iota(jnp.int32, (tq, LANES), 1)
    nt = (((1,), (1,)), ((), ()))
    first, last = pl.program_id(1) == 0, pl.program_id(1) == nq - 1

    def variant(j):
        lo, hi = (first if j == 0 else False), (last if j == nb - 1 else False)
        if lo is False and hi is False:
            return 1
        if hi is False:
            return jnp.where(lo, 0, 1)
        if lo is False:
            return jnp.where(hi, 2, 1)
        return jnp.where(lo & hi, 3, jnp.where(lo, 0, jnp.where(hi, 2, 1)))

    def residue(r, carry):
        heads = range(SEC // HEAD_DIM)
        sls = [slice(hh * HEAD_DIM, (hh + 1) * HEAD_DIM) for hh in heads]
        ks = [jnp.concatenate([kp_ref[r, :, sl], kc_ref[r, :, sl], kn_ref[r, :, sl]], axis=0) for sl in sls]
        vs = [jnp.concatenate([vp_ref[r, :, sl], vc_ref[r, :, sl], vn_ref[r, :, sl]], axis=0) for sl in sls]
        for j in range(nb):
            out_rows = pl.ds(j * tq * d + r, tq, stride=d) if d > 1 else pl.ds(j * tq, tq)
            keys = slice(j * tq, (j + 1) * tq + 2 * BAND)
            var = variant(j)
            lse_all = jnp.zeros((tq, LANES), F32)
            for hh in heads:
                q = q_ref[r, j * tq:(j + 1) * tq, sls[hh]]
                s = lax.dot_general(q, ks[hh][keys], nt, preferred_element_type=F32) + a_ref[var, hh]
                m = jnp.max(s, axis=-1, keepdims=True)
                p = jnp.exp(s - m)
                l = jnp.sum(p, axis=-1, keepdims=True)
                o = jnp.dot(p.astype(BF16), vs[hh][keys], preferred_element_type=F32)
                o_ref[hh, out_rows, :] = o / l
                lse_all = jnp.where(lane == hh, m + jnp.log(l), lse_all)
            lse_ref[out_rows, :] = lse_all
        return carry

    if d == 1:
        residue(0, 0)
    else:
        lax.fori_loop(0, d, residue, 0, unroll=unroll)


def _dilated_branch(rd, a_tab, b, s, d):
    ln = s // d
    tq = TQ_DIL
    rows, unroll = DIL_STEP[d]
    rows = min(rows, ln)
    nq = ln // rows
    edge = rows // BAND

    def cur(sec):
        return pl.BlockSpec((None, None, d, rows, SEC), lambda bi, qi, hg: (sec + hg, bi, 0, qi, 0))

    def prev(sec):
        return pl.BlockSpec((None, None, d, BAND, SEC),
                            lambda bi, qi, hg: (sec + hg, bi, 0, jnp.maximum(edge * qi - 1, 0), 0))

    def nxt(sec):
        return pl.BlockSpec((None, None, d, BAND, SEC),
                            lambda bi, qi, hg: (sec + hg, bi, 0, jnp.minimum(edge * (qi + 1), edge * nq - 1), 0))

    return pl.pallas_call(
        functools.partial(_dilated_kernel, nq=nq, unroll=unroll),
        grid=(b, nq, 2),
        in_specs=[cur(SEC_QA), prev(SEC_KA), cur(SEC_KA), nxt(SEC_KA), prev(SEC_VA), cur(SEC_VA), nxt(SEC_VA),
                  pl.BlockSpec((4, None, 4, tq, tq + 2 * BAND), lambda bi, qi, hg: (0, hg, 0, 0, 0))],
        out_specs=[pl.BlockSpec((None, 4, d * rows, HEAD_DIM), lambda bi, qi, hg: (bi, hg, qi, 0)),
                   pl.BlockSpec((None, None, d * rows, LANES), lambda bi, qi, hg: (hg, bi, qi, 0))],
        out_shape=[jax.ShapeDtypeStruct((b, N_HEADS_A, s, HEAD_DIM), F32),
                   jax.ShapeDtypeStruct((2, b, s, LANES), F32)],
        compiler_params=_cparams(3),
        name=f"dilated_d{d}",
    )(rd, rd, rd, rd, rd, rd, rd, a_tab)


def _t5_buckets(rel):
    nb = NUM_BUCKETS // 2
    max_exact = nb // 2
    n = np.abs(rel)
    large = max_exact + (np.log(np.maximum(n, 1) / max_exact) / np.log(MAX_DISTANCE / max_exact)
                         * (nb - max_exact)).astype(np.int32)
    large = np.minimum(large, nb - 1)
    return (rel > 0).astype(np.int32) * nb + np.where(n < max_exact, n, large).astype(np.int32)


def _band_tables(rel_bias, d):
    tq = TQ_DIL
    width = tq + 2 * BAND
    rel = np.arange(width + tq - 1) - (tq - 1) - BAND
    bucket = _t5_buckets(d * np.clip(rel, -BAND, BAND))
    diag = jnp.where(np.abs(rel) <= BAND, rel_bias.astype(F32)[bucket].T, NEG_INF)
    n = diag.shape[1]
    flat = jnp.tile(diag, (1, tq))[:, tq - 1:tq - 1 + tq * (n - 1)]
    table = flat.reshape(N_HEADS_A, tq, n - 1)[:, :, :width]
    c = np.arange(width)[None, None, :]
    prev_ok, next_ok = c >= BAND, c < BAND + tq
    tabs = [jnp.where(keep, table, NEG_INF) for keep in (prev_ok, c >= 0, next_ok, prev_ok & next_ok)]
    return jnp.stack(tabs).reshape(4, 2, 4, tq, width)


def _gqa_kernel(q_ref, k_ref, v_ref, o_ref, vx_ref, *, tk, unroll):
    tq = q_ref.shape[0]
    nk = k_ref.shape[0] // tk
    rep = SEC // HEAD_DIM

    @pl.when(pl.program_id(2) == 0)
    def _():
        vx_ref[:, :HEAD_DIM] = v_ref[...]
        vx_ref[:, HEAD_DIM:] = jnp.ones((vx_ref.shape[0], HEAD_DIM), BF16)

    q = jnp.concatenate([q_ref[:, hh * HEAD_DIM:(hh + 1) * HEAD_DIM] for hh in range(rep)], axis=0)
    nt = (((1,), (1,)), ((), ()))

    def body(c, carry):
        m, l, acc = carry
        start = pl.multiple_of(c * tk, tk)
        s = lax.dot_general(q, k_ref[pl.ds(start, tk), :], nt, preferred_element_type=F32)
        m_new = jnp.maximum(m, jnp.max(s, axis=-1, keepdims=True))
        alpha = jnp.exp2(m - m_new)
        p = jnp.exp2((s - m_new).astype(BF16))
        pv = jnp.dot(p, vx_ref[pl.ds(start, tk), :], preferred_element_type=F32)
        l = alpha * l + pv[:, HEAD_DIM:HEAD_DIM + 1]
        acc = alpha * acc + pv[:, :HEAD_DIM]
        return m_new, l, acc

    init = (jnp.full((rep * tq, 1), NEG_INF, F32), jnp.zeros((rep * tq, 1), F32),
            jnp.zeros((rep * tq, HEAD_DIM), F32))
    _, l, acc = lax.fori_loop(0, nk, body, init, unroll=unroll)
    o = acc / l
    for hh in range(rep):
        o_ref[:, hh * HEAD_DIM:(hh + 1) * HEAD_DIM] = o[hh * tq:(hh + 1) * tq]


def _gqa(qkv, b, s):
    tq = TQ_GQA
    tk = min(TK_GQA, s)
    view = qkv.reshape(N_SEC, b, s, SEC)
    o = pl.pallas_call(
        functools.partial(_gqa_kernel, tk=tk, unroll=min(UNROLL_GQA, s // tk)),
        grid=(b, N_KV_B, s // tq),
        in_specs=[pl.BlockSpec((None, None, tq, SEC), lambda bi, g, qi: (SEC_QB + g, bi, qi, 0)),
                  pl.BlockSpec((None, None, s, HEAD_DIM), lambda bi, g, qi: (SEC_KVB, bi, 0, g)),
                  pl.BlockSpec((None, None, s, HEAD_DIM), lambda bi, g, qi: (SEC_KVB, bi, 0, N_KV_B + g))],
        out_specs=pl.BlockSpec((None, tq, SEC), lambda bi, g, qi: (bi, qi, g)),
        out_shape=jax.ShapeDtypeStruct((b, s, WIDTH_B), F32),
        scratch_shapes=[pltpu.VMEM((s, 2 * HEAD_DIM), BF16)],
        compiler_params=_cparams(3),
        name="gqa",
    )(view, view, view)
    return o.reshape(b * s, WIDTH_B)


def _out_proj_kernel(o1_ref, o2_ref, o3_ref, l1_ref, l2_ref, l3_ref, ob_ref, x_ref, ga_ref, gb_ref, w_ref,
                     g2_ref, wr_ref, br_ref, y_ref, h_ref, meta_ref, cnt_ref, run_ref):
    @pl.when(pl.program_id(0) == 0)
    def _():
        run_ref[...] = jnp.zeros_like(run_ref)

    lses = (l1_ref[...], l2_ref[...], l3_ref[...])
    m = jnp.maximum(jnp.maximum(lses[0], lses[1]), lses[2])
    es = [jnp.exp(v - m) for v in lses]
    tot = es[0] + es[1] + es[2]
    ws = [e / tot for e in es]
    heads = []
    for h in range(N_HEADS_A):
        hg, hl = divmod(h, 4)
        acc = None
        for w, o_ref in zip(ws, (o1_ref, o2_ref, o3_ref)):
            term = w[hg][:, hl:hl + 1] * o_ref[h]
            acc = term if acc is None else acc + term
        heads.append(acc)
    oa = jnp.concatenate(heads, axis=1)
    oa = oa * lax.rsqrt(jnp.mean(oa * oa, axis=-1, keepdims=True) + EPS) * ga_ref[...]
    ob = ob_ref[...]
    ob = ob * lax.rsqrt(jnp.mean(ob * ob, axis=-1, keepdims=True) + EPS) * gb_ref[...]
    y = (jnp.dot(oa.astype(BF16), w_ref[:WIDTH_A, :], preferred_element_type=F32)
         + jnp.dot(ob.astype(BF16), w_ref[WIDTH_A:, :], preferred_element_type=F32))
    x2 = x_ref[...] + y
    y_ref[...] = x2
    h, meta, run = _route_tile(x2, g2_ref[...], wr_ref[...], br_ref[...], run_ref[...])
    h_ref[...] = _pack_halves(h)
    meta_ref[...] = meta
    run_ref[...] = run
    cnt_ref[...] = run


def _out_proj(os_, lses, ob, x, on_a, on_b, w_out, g2, w_r, b_r, s):
    t = x.shape[0]
    tm = TM_OUT
    spt = s // tm
    row = lambda width: pl.BlockSpec((tm, width), lambda i: (i, 0))
    o_spec = pl.BlockSpec((None, N_HEADS_A, tm, HEAD_DIM), lambda i: (i // spt, 0, i % spt, 0))
    lse_spec = pl.BlockSpec((2, None, tm, LANES), lambda i: (0, i // spt, i % spt, 0))
    const = lambda shape: pl.BlockSpec(shape, lambda i: (0, 0))
    return pl.pallas_call(
        _out_proj_kernel,
        grid=(t // tm,),
        in_specs=[o_spec, o_spec, o_spec, lse_spec, lse_spec, lse_spec, row(WIDTH_B),
                  row(D_MODEL), const((1, WIDTH_A)), const((1, WIDTH_B)), const((WIDTH_A + WIDTH_B, D_MODEL)),
                  const((1, D_MODEL)), const((D_MODEL, 2 * LANES)), const((1, 2 * LANES))],
        out_specs=[row(D_MODEL), row(D_PACK), row(LANES), const((1, LANES))],
        out_shape=[jax.ShapeDtypeStruct((t, D_MODEL), F32), jax.ShapeDtypeStruct((t, D_PACK), jnp.uint32),
                   jax.ShapeDtypeStruct((t, LANES), F32), jax.ShapeDtypeStruct((1, LANES), F32)],
        scratch_shapes=[pltpu.VMEM((1, LANES), F32)],
        compiler_params=_cparams(1),
        name="out_proj",
    )(*os_, *lses, ob, x, on_a, on_b, w_out, g2, w_r, b_r)


def _route_tile(x, g2, w_r, b_r, run):
    tm = x.shape[0]
    h = x * lax.rsqrt(jnp.mean(x * x, axis=-1, keepdims=True) + EPS) * g2
    logits = jnp.dot(h.astype(BF16), w_r, preferred_element_type=F32) + b_r
    gl, el = logits[:, :LANES], logits[:, LANES:]
    lane = lax.broadcasted_iota(jnp.int32, (tm, LANES), 1).astype(F32)
    ninf = jnp.float32(-jnp.inf)
    big = jnp.float32(LANES)

    def first_argmax(v):
        top = jnp.max(v, axis=-1, keepdims=True)
        return top, jnp.min(jnp.where(v == top, lane, big), axis=-1, keepdims=True)

    gmask = lane < N_GROUPS
    gtop, gsel = first_argmax(jnp.where(gmask, gl, ninf))
    g_w = 1.0 / jnp.sum(jnp.where(gmask, jnp.exp(gl - gtop), 0.0), axis=-1, keepdims=True)
    lo = gsel * EXPERTS_PER_GROUP
    elm = jnp.where((lane >= lo) & (lane < lo + EXPERTS_PER_GROUP), el, ninf)
    t1, i1 = first_argmax(elm)
    t2, i2 = first_argmax(jnp.where(lane == i1, ninf, elm))
    e2 = jnp.exp(t2 - t1)
    gate1 = g_w * (1.0 / (1.0 + e2))
    gate2 = g_w * (e2 / (1.0 + e2))

    hot1, hot2 = lane == i1, lane == i2
    hot = (hot1 | hot2).astype(F32)
    r_i = lax.broadcasted_iota(jnp.int32, (tm, tm), 0)
    c_i = lax.broadcasted_iota(jnp.int32, (tm, tm), 1)
    before = (c_i < r_i).astype(BF16)
    rank = jnp.dot(before, hot.astype(BF16), preferred_element_type=F32) + run
    rank1 = jnp.sum(jnp.where(hot1, rank, 0.0), axis=-1, keepdims=True)
    rank2 = jnp.sum(jnp.where(hot2, rank, 0.0), axis=-1, keepdims=True)
    run = run + jnp.sum(hot, axis=0, keepdims=True)

    meta = jnp.zeros((tm, LANES), F32)
    for k, v in enumerate((i1, i2, rank1, rank2, gate1, gate2)):
        meta = jnp.where(lane == k, v, meta)
    return h, meta, run


def _dispatch_kernel(plan_ref, dest_ref, h_ref, xs_ref, zero_ref, zsem, sem):
    i = pl.program_id(0)
    tt = dest_ref.shape[2] // TOP_K
    rows = zero_ref.shape[0]
    p_rows = xs_ref.shape[0]

    def pad_block(e):
        return plan_ref[N_EXPERTS + e] - rows, plan_ref[e] > 0

    def tail_block(c):
        row0 = plan_ref[2 * N_EXPERTS - 1] + c * rows
        return row0, row0 < p_rows

    def for_zero_blocks(block, action):
        def body(c, carry):
            row0, live = block(c)

            @pl.when(live)
            def _():
                dst = xs_ref.at[pl.ds(pl.multiple_of(row0, rows), rows)]
                action(pltpu.make_async_copy(zero_ref, dst, zsem))
            return carry

        lax.fori_loop(0, N_EXPERTS, body, 0)

    @pl.when(i == 0)
    def _():
        zero_ref[...] = jnp.zeros_like(zero_ref)
        for block in (pad_block, tail_block):
            for_zero_blocks(block, lambda cp: cp.start())
        for block in (pad_block, tail_block):
            for_zero_blocks(block, lambda cp: cp.wait())

    def row_copy(tok, k):
        dst = dest_ref[0, 0, TOP_K * tok + k]
        return pltpu.make_async_copy(h_ref.at[pl.ds(tok, 1)], xs_ref.at[pl.ds(dst, 1)], sem)

    def start_rows(tok, c):
        for k in range(TOP_K):
            row_copy(tok, k).start()
        return c

    def wait_rows(tok, c):
        for k in range(TOP_K):
            row_copy(tok, k).wait()
        return c

    lax.fori_loop(0, tt, start_rows, 0, unroll=ROW_UNROLL)
    lax.fori_loop(0, tt, wait_rows, 0, unroll=ROW_UNROLL)


def _dispatch(plan, dest, h2, p_rows):
    t = h2.shape[0]
    tt = TT_ROWS
    return pl.pallas_call(
        _dispatch_kernel,
        grid_spec=pltpu.PrefetchScalarGridSpec(
            num_scalar_prefetch=1,
            grid=(t // tt,),
            in_specs=[pl.BlockSpec((1, 1, TOP_K * tt), lambda i, plan: (i, 0, 0), memory_space=pltpu.SMEM),
                      pl.BlockSpec((tt, D_PACK), lambda i, plan: (i, 0))],
            out_specs=pl.BlockSpec(memory_space=pl.ANY),
            scratch_shapes=[pltpu.VMEM((MOE_ROWS, D_PACK), jnp.uint32), pltpu.SemaphoreType.DMA(()),
                            pltpu.SemaphoreType.DMA(())]),
        out_shape=jax.ShapeDtypeStruct((p_rows, D_PACK), jnp.uint32),
        compiler_params=_cparams(1),
        name="dispatch",
    )(plan, dest, h2)


def _experts_kernel(sched_ref, nused_ref, xs_ref, wg_hbm, wu_hbm, wd_hbm, ys_ref, wg_buf, wu_buf, wd_buf, sems):
    i = pl.program_id(0)
    live = i < nused_ref[0]
    nblk = pl.num_programs(0)
    expert, first, slot, nxt = (sched_ref[r * nblk + i] for r in range(4))

    def weight_copies(e, sl):
        return [pltpu.make_async_copy(src.at[e], buf.at[sl], sems.at[sl])
                for src, buf in ((wg_hbm, wg_buf), (wu_hbm, wu_buf), (wd_hbm, wd_buf))]

    @pl.when(i == 0)
    def _():
        for cp in weight_copies(expert, slot):
            cp.start()

    @pl.when(live & (first == 1))
    def _():
        for cp in weight_copies(expert, slot):
            cp.wait()

        @pl.when(nxt >= 0)
        def _():
            for cp in weight_copies(nxt, 1 - slot):
                cp.start()

    @pl.when(live)
    def _():
        lo, hi = (v.astype(BF16) for v in _unpack_halves(xs_ref[...]))
        up = lambda w: (jnp.dot(lo, w[slot, :D_PACK, :], preferred_element_type=F32)
                        + jnp.dot(hi, w[slot, D_PACK:, :], preferred_element_type=F32))
        g, u = up(wg_buf), up(wu_buf)
        hid = g * (1.0 / (1.0 + jnp.exp(-g))) * u
        ys_ref[...] = _pack_halves(jnp.dot(hid.astype(BF16), wd_buf[slot], preferred_element_type=F32))

    @pl.when(jnp.logical_not(live))
    def _():
        ys_ref[...] = jnp.zeros_like(ys_ref)


def _experts(sched, nused, xs, w_gate, w_up, w_down):
    p_rows = xs.shape[0]
    rows = MOE_ROWS
    live = lambda i, sched, nused: jnp.minimum(i, jnp.maximum(nused[0] - 1, 0))
    hbm = pl.BlockSpec(memory_space=pl.ANY)
    return pl.pallas_call(
        _experts_kernel,
        grid_spec=pltpu.PrefetchScalarGridSpec(
            num_scalar_prefetch=2,
            grid=(p_rows // rows,),
            in_specs=[pl.BlockSpec((rows, D_PACK), lambda i, sched, nused: (live(i, sched, nused), 0)), hbm, hbm, hbm],
            out_specs=pl.BlockSpec((rows, D_PACK), lambda i, sched, nused: (i, 0)),
            scratch_shapes=[pltpu.VMEM((2, D_MODEL, D_EXPERT), BF16), pltpu.VMEM((2, D_MODEL, D_EXPERT), BF16),
                            pltpu.VMEM((2, D_EXPERT, D_MODEL), BF16), pltpu.SemaphoreType.DMA((2,))]),
        out_shape=jax.ShapeDtypeStruct((p_rows, D_PACK), jnp.uint32),
        compiler_params=_cparams(1),
        name="experts",
    )(sched, nused, xs, w_gate, w_up, w_down)


def _combine_kernel(dest_ref, x_ref, meta_ref, ys_ref, o_ref, buf_ref, sem):
    tt = x_ref.shape[0]

    def row_copy(tok, k):
        src = dest_ref[0, 0, TOP_K * tok + k]
        return pltpu.make_async_copy(ys_ref.at[pl.ds(src, 1)], buf_ref.at[k, pl.ds(tok, 1)], sem)

    def start_rows(tok, c):
        for k in range(TOP_K):
            row_copy(tok, k).start()
        return c

    def wait_rows(tok, c):
        for k in range(TOP_K):
            row_copy(tok, k).wait()
        return c

    lax.fori_loop(0, tt, start_rows, 0, unroll=ROW_UNROLL)
    lax.fori_loop(0, tt, wait_rows, 0, unroll=ROW_UNROLL)
    meta = meta_ref[...]
    g1, g2 = meta[:, 4:5], meta[:, 5:6]
    (lo1, hi1), (lo2, hi2) = _unpack_halves(buf_ref[0]), _unpack_halves(buf_ref[1])
    o_ref[:, :D_PACK] = x_ref[:, :D_PACK] + (g1 * lo1 + g2 * lo2)
    o_ref[:, D_PACK:] = x_ref[:, D_PACK:] + (g1 * hi1 + g2 * hi2)


def _combine(dest, x2, meta, ys):
    t = x2.shape[0]
    tt = TT_ROWS
    return pl.pallas_call(
        _combine_kernel,
        grid=(t // tt,),
        in_specs=[pl.BlockSpec((1, 1, TOP_K * tt), lambda i: (i, 0, 0), memory_space=pltpu.SMEM),
                  pl.BlockSpec((tt, D_MODEL), lambda i: (i, 0)),
                  pl.BlockSpec((tt, LANES), lambda i: (i, 0)),
                  pl.BlockSpec(memory_space=pl.ANY)],
        out_specs=pl.BlockSpec((tt, D_MODEL), lambda i: (i, 0)),
        out_shape=jax.ShapeDtypeStruct((t, D_MODEL), F32),
        scratch_shapes=[pltpu.VMEM((TOP_K, tt, D_PACK), jnp.uint32), pltpu.SemaphoreType.DMA(())],
        compiler_params=_cparams(1),
        name="combine",
    )(dest, x2, meta, ys)


def _rope_tables(s):
    rows = s // GRID_W
    r, c = jnp.meshgrid(jnp.arange(rows, dtype=F32), jnp.arange(GRID_W, dtype=F32), indexing='ij')
    n_freq = HEAD_DIM // 4
    inv = ROPE_THETA ** (-jnp.arange(n_freq, dtype=F32) / n_freq)
    ang_r = r.reshape(-1)[:, None] * inv
    ang_c = c.reshape(-1)[:, None] * inv
    ang = jnp.concatenate([ang_r, ang_r, ang_c, ang_c], axis=-1)
    sign = np.tile(np.repeat(np.array([-1.0, 1.0], np.float32), n_freq), 2)
    return jnp.cos(ang), jnp.sin(ang) * sign


def _moe_plan(meta, cnt, t):
    rows = MOE_ROWS
    p_rows = TOP_K * t + N_EXPERTS * rows
    nblk = p_rows // rows
    counts = cnt[0, :N_EXPERTS].astype(jnp.int32)
    pcounts = (counts + rows - 1) // rows * rows
    pends = jnp.cumsum(pcounts)
    pstarts = pends - pcounts
    expert = meta[:, 0:TOP_K].astype(jnp.int32)
    rank = meta[:, TOP_K:2 * TOP_K].astype(jnp.int32)
    ids = jnp.arange(N_EXPERTS, dtype=jnp.int32)
    start = jnp.sum(jnp.where(expert[..., None] == ids, pstarts, 0), axis=-1)
    dest = (start + rank).reshape(t // TT_ROWS, 1, TOP_K * TT_ROWS)
    nused = pends[-1] // rows
    row0 = jnp.arange(nblk, dtype=jnp.int32) * rows
    blk = jnp.sum(pends[None, :] <= jnp.minimum(row0, pends[-1] - rows)[:, None], axis=1).astype(jnp.int32)
    plan = jnp.concatenate([pcounts, pends]).astype(jnp.int32)
    blocks = jnp.arange(nblk, dtype=jnp.int32)
    first = ((blocks == 0) | (blk != jnp.roll(blk, 1))) & (blocks < nused)
    slot = (jnp.cumsum(first.astype(jnp.int32)) - 1) % 2
    later = (pcounts[None, :] > 0) & (ids[None, :] > ids[:, None])
    next_used = jnp.min(jnp.where(later, ids[None, :], N_EXPERTS), axis=1)
    next_used = jnp.where(next_used == N_EXPERTS, -1, next_used)
    nxt = jnp.sum(jnp.where(blk[:, None] == ids[None, :], next_used[None, :], 0), axis=1)
    sched = jnp.concatenate([blk, first.astype(jnp.int32), slot, nxt]).astype(jnp.int32)
    return plan, dest, sched, nused.reshape(1).astype(jnp.int32), p_rows


def _trunk(x, p):
    b, s, _ = x.shape
    t = b * s
    assert s % (TQ_DIL * max(d for _, d in DILATED)) == 0 and s % TM_IN == 0 and t % TT_ROWS == 0
    x = x.reshape(t, D_MODEL)
    cos, sin = _rope_tables(s)
    qkv, r4, r16 = _in_proj(x, p['norm1_g'], p['w_in'], p['gain'], cos, sin, b, s)
    residue_major = {1: qkv.reshape(N_SEC, b, 1, s, SEC), 4: r4, 16: r16}
    branches = [_dilated_branch(residue_major[d], p['band'][i], b, s, d) for i, (_, d) in enumerate(DILATED)]
    ob = _gqa(qkv, b, s)
    x2, h2, meta, cnt = _out_proj([o for o, _ in branches], [l for _, l in branches], ob, x, p['on_a'], p['on_b'],
                                  p['w_out'], p['norm2_g'], p['w_r'], p['b_r'], s)
    plan, dest, sched, nused, p_rows = _moe_plan(meta, cnt, t)
    xs = _dispatch(plan, dest, h2, p_rows)
    ys = _experts(sched, nused, xs, p['w_gate'], p['w_up'], p['w_down'])
    y = _combine(dest, x2, meta, ys)
    return y.reshape(b, s, D_MODEL)


def kernel(x_prompt, x_sample, norm1_g, w_in, qn_a, kn_a, qn_b, kn_b, rel_bias, on_a, on_b, w_out, norm2_g, rg_w,
           rg_b, re_w, re_b, w_gate, w_up, w_down):
    assert norm1_g.shape[0] == 1, "one layer"
    scale = HEAD_DIM ** -0.5
    ones_a = jnp.ones((WIDTH_A,), F32)
    gain = jnp.concatenate([jnp.tile(qn_a[0], N_HEADS_A) * scale, jnp.tile(kn_a[0], N_HEADS_A), ones_a,
                            jnp.tile(qn_b[0], N_HEADS_B) * (scale * LOG2E), jnp.tile(kn_b[0], N_KV_B),
                            jnp.ones((KV_WIDTH_B,), F32)]).reshape(1, IN_WIDTH)
    pad_g = jnp.zeros((D_MODEL, LANES - N_GROUPS), F32)
    pad_e = jnp.zeros((D_MODEL, LANES - N_EXPERTS), F32)
    params = dict(
        norm1_g=norm1_g, w_in=w_in[0].astype(BF16), gain=gain,
        band=[_band_tables(rel_bias, d) for _, d in DILATED],
        on_a=on_a, on_b=on_b, w_out=w_out[0].astype(BF16), norm2_g=norm2_g,
        w_r=jnp.concatenate([rg_w[0], pad_g, re_w[0], pad_e], axis=1).astype(BF16),
        b_r=jnp.concatenate([rg_b[0], jnp.zeros((LANES - N_GROUPS,), F32), re_b[0],
                             jnp.zeros((LANES - N_EXPERTS,), F32)]).reshape(1, 2 * LANES),
        w_gate=w_gate[0].astype(BF16), w_up=w_up[0].astype(BF16), w_down=w_down[0].astype(BF16))
    return _trunk(x_prompt, params), _trunk(x_sample, params)
```

```python
import functools

import numpy as np
import jax
import jax.numpy as jnp
from jax import lax
from jax.experimental import pallas as pl
from jax.experimental.pallas import tpu as pltpu

D_MODEL = 2048
HEAD_DIM = 128
N_HEADS_A = 8
N_HEADS_B = 8
N_KV_B = 2
WIDTH_A = N_HEADS_A * HEAD_DIM
WIDTH_B = N_HEADS_B * HEAD_DIM
KV_WIDTH_B = N_KV_B * HEAD_DIM
IN_WIDTH = 3 * WIDTH_A + WIDTH_B + 2 * KV_WIDTH_B
DILATED = ((128, 1), (512, 4), (2048, 16))
GRID_W = 64
ROPE_THETA = 10000.0
NUM_BUCKETS = 32
MAX_DISTANCE = 1024
N_GROUPS = 4
EXPERTS_PER_GROUP = 8
N_EXPERTS = N_GROUPS * EXPERTS_PER_GROUP
TOP_K = 2
D_EXPERT = D_MODEL // 2
D_PACK = D_MODEL // 2
EPS = 1e-6
NEG_INF = -1e30
LOG2E = float(np.log2(np.e))

LANES = 128
SEC = 4 * HEAD_DIM
N_SEC = IN_WIDTH // SEC
SEC_GROUP = 3
SEC_QA, SEC_KA, SEC_VA, SEC_QB, SEC_KVB = 0, 2, 4, 6, 8
BAND = 64
VMEM_LIMIT = 56 * 1024 * 1024

TM_IN = 512
TQ_DIL = 128
DIL_STEP = {1: (1024, 1), 4: (256, 4), 16: (128, 8)}
TQ_GQA = 256
TK_GQA = 512
UNROLL_GQA = 16
TM_OUT = 256
TT_ROWS = 256
ROW_UNROLL = 8
MOE_ROWS = 256
CAST_ROWS = 256

F32 = jnp.float32
BF16 = jnp.bfloat16


def _cparams(n_axes):
    return pltpu.CompilerParams(dimension_semantics=("arbitrary",) * n_axes, vmem_limit_bytes=VMEM_LIMIT)


def _pack_halves(a):
    n = a.shape[1] // 2
    lo = pltpu.bitcast(a[:, :n].astype(BF16).astype(F32), jnp.uint32)
    hi = pltpu.bitcast(a[:, n:].astype(BF16).astype(F32), jnp.uint32)
    return (lo >> 16) | (hi & jnp.uint32(0xFFFF0000))


def _unpack_halves(u):
    return pltpu.bitcast(u << 16, F32), pltpu.bitcast(u & jnp.uint32(0xFFFF0000), F32)


def _in_proj_kernel(x_ref, g1_ref, w_ref, gain_ref, cos_ref, sin_ref, o_ref, r4_ref, r16_ref, h_ref, y_ref, y4_ref):
    jp = pl.program_id(1)
    tm = x_ref.shape[0]
    heads = SEC // HEAD_DIM

    @pl.when(jp == 0)
    def _():
        x = x_ref[...]
        ms = jnp.mean(x * x, axis=-1, keepdims=True)
        h_ref[...] = (x * lax.rsqrt(ms + EPS) * g1_ref[...]).astype(BF16)

    lane = lax.broadcasted_iota(jnp.int32, (tm, HEAD_DIM), 1)
    first_quarter = (lane & 32) == 0

    def section(half, norm, rope, residue_major=False):
        p = jnp.dot(h_ref[...], w_ref[:, half * SEC:(half + 1) * SEC], preferred_element_type=F32)
        for hh in range(heads):
            sl = slice(hh * HEAD_DIM, (hh + 1) * HEAD_DIM)
            y = p[:, sl]
            if norm[hh]:
                ms = jnp.mean(y * y, axis=-1, keepdims=True)
                y = y * lax.rsqrt(ms + EPS)
            y = y * gain_ref[:, half * SEC + hh * HEAD_DIM:half * SEC + (hh + 1) * HEAD_DIM]
            if rope[hh]:
                partner = jnp.where(first_quarter, pltpu.roll(y, HEAD_DIM - 32, 1), pltpu.roll(y, 32, 1))
                y = y * cos_ref[...] + partner * sin_ref[...]
            o_ref[half, :, sl] = y.astype(BF16)
            if residue_major:
                slab = half * heads + hh
                y_ref[slab] = y
                for r in range(4):
                    y4 = y_ref[slab, pl.ds(r, tm // 4, stride=4), :]
                    r4_ref[half, r, :, sl] = y4.astype(BF16)
                    y4_ref[slab, pl.ds(r * (tm // 4), tm // 4), :] = y4
                for r in range(4):
                    for q in range(4):
                        y16 = y4_ref[slab, pl.ds(r * (tm // 4) + q, tm // 16, stride=4), :]
                        r16_ref[half, 4 * q + r, :, sl] = y16.astype(BF16)

    yes, no = (True,) * heads, (False,) * heads
    kv_b = (True,) * N_KV_B + (False,) * (heads - N_KV_B)

    def flags(sec):
        if sec < SEC_VA:
            return yes, no, True
        if sec < SEC_QB:
            return no, no, True
        if sec < SEC_KVB:
            return yes, yes, False
        return kv_b, kv_b, False

    for group in range(N_SEC // SEC_GROUP):
        @pl.when(jp == group)
        def _(group=group):
            for part in range(SEC_GROUP):
                section(part, *flags(group * SEC_GROUP + part))


def _in_proj(x, g1, w_in, gain, cos, sin, b, s):
    t = x.shape[0]
    tm = TM_IN
    spt = s // tm
    grp = SEC_GROUP
    n_dil = SEC_QB // grp

    def residue_major(d):
        spec = pl.BlockSpec((grp, None, d, tm // d, SEC),
                            lambda i, jp: (jnp.minimum(jp, n_dil - 1), i // spt, 0, i % spt, 0))
        return spec, jax.ShapeDtypeStruct((grp * n_dil, b, d, s // d, SEC), BF16)

    (r4_spec, r4_shape), (r16_spec, r16_shape) = residue_major(4), residue_major(16)
    return pl.pallas_call(
        _in_proj_kernel,
        grid=(t // tm, N_SEC // grp),
        in_specs=[
            pl.BlockSpec((tm, D_MODEL), lambda i, jp: (i, 0)),
            pl.BlockSpec((1, D_MODEL), lambda i, jp: (0, 0)),
            pl.BlockSpec((D_MODEL, grp * SEC), lambda i, jp: (0, jp)),
            pl.BlockSpec((1, grp * SEC), lambda i, jp: (0, jp)),
            pl.BlockSpec((tm, HEAD_DIM), lambda i, jp: (i % spt, 0)),
            pl.BlockSpec((tm, HEAD_DIM), lambda i, jp: (i % spt, 0)),
        ],
        out_specs=[pl.BlockSpec((grp, tm, SEC), lambda i, jp: (jp, i, 0)), r4_spec, r16_spec],
        out_shape=[jax.ShapeDtypeStruct((N_SEC, t, SEC), BF16), r4_shape, r16_shape],
        scratch_shapes=[pltpu.VMEM((tm, D_MODEL), BF16), pltpu.VMEM((grp * SEC // HEAD_DIM, tm, HEAD_DIM), F32),
                        pltpu.VMEM((grp * SEC // HEAD_DIM, tm, HEAD_DIM), F32)],
        compiler_params=_cparams(2),
        name="in_proj",
    )(x, g1, w_in, gain, cos, sin)


def _dilated_kernel(q_ref, kp_ref, kc_ref, kn_ref, vp_ref, vc_ref, vn_ref, a_ref, o_ref, lse_ref, *, nq, unroll):
    d, rows = q_ref.shape[0], q_ref.shape[1]
    tq = TQ_DIL
    nb = rows // tq
    lane = lax.broadcasted_iota(jnp.int32, (tq, LANES), 1)
    nt = (((1,), (1,)), ((), ()))
    first, last = pl.program_id(1) == 0, pl.program_id(1) == nq - 1

    def variant(j):
        lo, hi = (first if j == 0 else False), (last if j == nb - 1 else False)
        if lo is False and hi is False:
            return 1
        if hi is False:
            return jnp.where(lo, 0, 1)
        if lo is False:
            return jnp.where(hi, 2, 1)
        return jnp.where(lo & hi, 3, jnp.where(lo, 0, jnp.where(hi, 2, 1)))

    def residue(r, carry):
        heads = range(SEC // HEAD_DIM)
        sls = [slice(hh * HEAD_DIM, (hh + 1) * HEAD_DIM) for hh in heads]
        ks = [jnp.concatenate([kp_ref[r, :, sl], kc_ref[r, :, sl], kn_ref[r, :, sl]], axis=0) for sl in sls]
        vs = [jnp.concatenate([vp_ref[r, :, sl], vc_ref[r, :, sl], vn_ref[r, :, sl]], axis=0) for sl in sls]
        for j in range(nb):
            out_rows = pl.ds(j * tq * d + r, tq, stride=d) if d > 1 else pl.ds(j * tq, tq)
            keys = slice(j * tq, (j + 1) * tq + 2 * BAND)
            var = variant(j)
            lse_all = jnp.zeros((tq, LANES), F32)
            for hh in heads:
                q = q_ref[r, j * tq:(j + 1) * tq, sls[hh]]
                s = lax.dot_general(q, ks[hh][keys], nt, preferred_element_type=F32) + a_ref[var, hh]
                m = jnp.max(s, axis=-1, keepdims=True)
                p = jnp.exp(s - m)
                l = jnp.sum(p, axis=-1, keepdims=True)
                o = jnp.dot(p.astype(BF16), vs[hh][keys], preferred_element_type=F32)
                o_ref[hh, out_rows, :] = o / l
                lse_all = jnp.where(lane == hh, m + jnp.log(l), lse_all)
            lse_ref[out_rows, :] = lse_all
        return carry

    if d == 1:
        residue(0, 0)
    else:
        lax.fori_loop(0, d, residue, 0, unroll=unroll)


def _dilated_branch(rd, a_tab, b, s, d):
    ln = s // d
    tq = TQ_DIL
    rows, unroll = DIL_STEP[d]
    rows = min(rows, ln)
    nq = ln // rows
    edge = rows // BAND

    def cur(sec):
        return pl.BlockSpec((None, None, d, rows, SEC), lambda bi, qi, hg: (sec + hg, bi, 0, qi, 0))

    def prev(sec):
        return pl.BlockSpec((None, None, d, BAND, SEC),
                            lambda bi, qi, hg: (sec + hg, bi, 0, jnp.maximum(edge * qi - 1, 0), 0))

    def nxt(sec):
        return pl.BlockSpec((None, None, d, BAND, SEC),
                            lambda bi, qi, hg: (sec + hg, bi, 0, jnp.minimum(edge * (qi + 1), edge * nq - 1), 0))

    return pl.pallas_call(
        functools.partial(_dilated_kernel, nq=nq, unroll=unroll),
        grid=(b, nq, 2),
        in_specs=[cur(SEC_QA), prev(SEC_KA), cur(SEC_KA), nxt(SEC_KA), prev(SEC_VA), cur(SEC_VA), nxt(SEC_VA),
                  pl.BlockSpec((4, None, 4, tq, tq + 2 * BAND), lambda bi, qi, hg: (0, hg, 0, 0, 0))],
        out_specs=[pl.BlockSpec((None, 4, d * rows, HEAD_DIM), lambda bi, qi, hg: (bi, hg, qi, 0)),
                   pl.BlockSpec((None, None, d * rows, LANES), lambda bi, qi, hg: (hg, bi, qi, 0))],
        out_shape=[jax.ShapeDtypeStruct((b, N_HEADS_A, s, HEAD_DIM), F32),
                   jax.ShapeDtypeStruct((2, b, s, LANES), F32)],
        compiler_params=_cparams(3),
        name=f"dilated_d{d}",
    )(rd, rd, rd, rd, rd, rd, rd, a_tab)


def _t5_buckets(rel):
    nb = NUM_BUCKETS // 2
    max_exact = nb // 2
    n = np.abs(rel)
    large = max_exact + (np.log(np.maximum(n, 1) / max_exact) / np.log(MAX_DISTANCE / max_exact)
                         * (nb - max_exact)).astype(np.int32)
    large = np.minimum(large, nb - 1)
    return (rel > 0).astype(np.int32) * nb + np.where(n < max_exact, n, large).astype(np.int32)


def _band_tables(rel_bias, d):
    tq = TQ_DIL
    width = tq + 2 * BAND
    rel = np.arange(width + tq - 1) - (tq - 1) - BAND
    bucket = _t5_buckets(d * np.clip(rel, -BAND, BAND))
    diag = jnp.where(np.abs(rel) <= BAND, rel_bias.astype(F32)[bucket].T, NEG_INF)
    n = diag.shape[1]
    flat = jnp.tile(diag, (1, tq))[:, tq - 1:tq - 1 + tq * (n - 1)]
    table = flat.reshape(N_HEADS_A, tq, n - 1)[:, :, :width]
    c = np.arange(width)[None, None, :]
    prev_ok, next_ok = c >= BAND, c < BAND + tq
    tabs = [jnp.where(keep, table, NEG_INF) for keep in (prev_ok, c >= 0, next_ok, prev_ok & next_ok)]
    return jnp.stack(tabs).reshape(4, 2, 4, tq, width)


def _gqa_kernel(q_ref, k_ref, v_ref, o_ref, vx_ref, *, tk, unroll):
    tq = q_ref.shape[0]
    nk = k_ref.shape[0] // tk
    rep = SEC // HEAD_DIM

    @pl.when(pl.program_id(2) == 0)
    def _():
        vx_ref[:, :HEAD_DIM] = v_ref[...]
        vx_ref[:, HEAD_DIM:] = jnp.ones((vx_ref.shape[0], HEAD_DIM), BF16)

    q = jnp.concatenate([q_ref[:, hh * HEAD_DIM:(hh + 1) * HEAD_DIM] for hh in range(rep)], axis=0)
    nt = (((1,), (1,)), ((), ()))

    def body(c, carry):
        m, l, acc = carry
        start = pl.multiple_of(c * tk, tk)
        s = lax.dot_general(q, k_ref[pl.ds(start, tk), :], nt, preferred_element_type=F32)
        m_new = jnp.maximum(m, jnp.max(s, axis=-1, keepdims=True))
        alpha = jnp.exp2(m - m_new)
        p = jnp.exp2((s - m_new).astype(BF16))
        pv = jnp.dot(p, vx_ref[pl.ds(start, tk), :], preferred_element_type=F32)
        l = alpha * l + pv[:, HEAD_DIM:HEAD_DIM + 1]
        acc = alpha * acc + pv[:, :HEAD_DIM]
        return m_new, l, acc

    init = (jnp.full((rep * tq, 1), NEG_INF, F32), jnp.zeros((rep * tq, 1), F32),
            jnp.zeros((rep * tq, HEAD_DIM), F32))
    _, l, acc = lax.fori_loop(0, nk, body, init, unroll=unroll)
    o = acc / l
    for hh in range(rep):
        o_ref[:, hh * HEAD_DIM:(hh + 1) * HEAD_DIM] = o[hh * tq:(hh + 1) * tq]


def _gqa(qkv, b, s):
    tq = TQ_GQA
    tk = min(TK_GQA, s)
    view = qkv.reshape(N_SEC, b, s, SEC)
    o = pl.pallas_call(
        functools.partial(_gqa_kernel, tk=tk, unroll=min(UNROLL_GQA, s // tk)),
        grid=(b, N_KV_B, s // tq),
        in_specs=[pl.BlockSpec((None, None, tq, SEC), lambda bi, g, qi: (SEC_QB + g, bi, qi, 0)),
                  pl.BlockSpec((None, None, s, HEAD_DIM), lambda bi, g, qi: (SEC_KVB, bi, 0, g)),
                  pl.BlockSpec((None, None, s, HEAD_DIM), lambda bi, g, qi: (SEC_KVB, bi, 0, N_KV_B + g))],
        out_specs=pl.BlockSpec((None, tq, SEC), lambda bi, g, qi: (bi, qi, g)),
        out_shape=jax.ShapeDtypeStruct((b, s, WIDTH_B), F32),
        scratch_shapes=[pltpu.VMEM((s, 2 * HEAD_DIM), BF16)],
        compiler_params=_cparams(3),
        name="gqa",
    )(view, view, view)
    return o.reshape(b * s, WIDTH_B)


def _out_proj_kernel(o1_ref, o2_ref, o3_ref, l1_ref, l2_ref, l3_ref, ob_ref, x_ref, ga_ref, gb_ref, w_ref,
                     g2_ref, wr_ref, br_ref, y_ref, h_ref, meta_ref, cnt_ref, run_ref):
    @pl.when(pl.program_id(0) == 0)
    def _():
        run_ref[...] = jnp.zeros_like(run_ref)

    lses = (l1_ref[...], l2_ref[...], l3_ref[...])
    m = jnp.maximum(jnp.maximum(lses[0], lses[1]), lses[2])
    es = [jnp.exp(v - m) for v in lses]
    tot = es[0] + es[1] + es[2]
    ws = [e / tot for e in es]
    heads = []
    for h in range(N_HEADS_A):
        hg, hl = divmod(h, 4)
        acc = None
        for w, o_ref in zip(ws, (o1_ref, o2_ref, o3_ref)):
            term = w[hg][:, hl:hl + 1] * o_ref[h]
            acc = term if acc is None else acc + term
        heads.append(acc)
    oa = jnp.concatenate(heads, axis=1)
    oa = oa * lax.rsqrt(jnp.mean(oa * oa, axis=-1, keepdims=True) + EPS) * ga_ref[...]
    ob = ob_ref[...]
    ob = ob * lax.rsqrt(jnp.mean(ob * ob, axis=-1, keepdims=True) + EPS) * gb_ref[...]
    y = (jnp.dot(oa.astype(BF16), w_ref[:WIDTH_A, :], preferred_element_type=F32)
         + jnp.dot(ob.astype(BF16), w_ref[WIDTH_A:, :], preferred_element_type=F32))
    x2 = x_ref[...] + y
    y_ref[...] = x2
    h, meta, run = _route_tile(x2, g2_ref[...], wr_ref[...], br_ref[...], run_ref[...])
    h_ref[...] = _pack_halves(h)
    meta_ref[...] = meta
    run_ref[...] = run
    cnt_ref[...] = run


def _out_proj(os_, lses, ob, x, on_a, on_b, w_out, g2, w_r, b_r, s):
    t = x.shape[0]
    tm = TM_OUT
    spt = s // tm
    row = lambda width: pl.BlockSpec((tm, width), lambda i: (i, 0))
    o_spec = pl.BlockSpec((None, N_HEADS_A, tm, HEAD_DIM), lambda i: (i // spt, 0, i % spt, 0))
    lse_spec = pl.BlockSpec((2, None, tm, LANES), lambda i: (0, i // spt, i % spt, 0))
    const = lambda shape: pl.BlockSpec(shape, lambda i: (0, 0))
    return pl.pallas_call(
        _out_proj_kernel,
        grid=(t // tm,),
        in_specs=[o_spec, o_spec, o_spec, lse_spec, lse_spec, lse_spec, row(WIDTH_B),
                  row(D_MODEL), const((1, WIDTH_A)), const((1, WIDTH_B)), const((WIDTH_A + WIDTH_B, D_MODEL)),
                  const((1, D_MODEL)), const((D_MODEL, 2 * LANES)), const((1, 2 * LANES))],
        out_specs=[row(D_MODEL), row(D_PACK), row(LANES), const((1, LANES))],
        out_shape=[jax.ShapeDtypeStruct((t, D_MODEL), F32), jax.ShapeDtypeStruct((t, D_PACK), jnp.uint32),
                   jax.ShapeDtypeStruct((t, LANES), F32), jax.ShapeDtypeStruct((1, LANES), F32)],
        scratch_shapes=[pltpu.VMEM((1, LANES), F32)],
        compiler_params=_cparams(1),
        name="out_proj",
    )(*os_, *lses, ob, x, on_a, on_b, w_out, g2, w_r, b_r)


def _route_tile(x, g2, w_r, b_r, run):
    tm = x.shape[0]
    h = x * lax.rsqrt(jnp.mean(x * x, axis=-1, keepdims=True) + EPS) * g2
    logits = jnp.dot(h.astype(BF16), w_r, preferred_element_type=F32) + b_r
    gl, el = logits[:, :LANES], logits[:, LANES:]
    lane = lax.broadcasted_iota(jnp.int32, (tm, LANES), 1).astype(F32)
    ninf = jnp.float32(-jnp.inf)
    big = jnp.float32(LANES)

    def first_argmax(v):
        top = jnp.max(v, axis=-1, keepdims=True)
        return top, jnp.min(jnp.where(v == top, lane, big), axis=-1, keepdims=True)

    gmask = lane < N_GROUPS
    gtop, gsel = first_argmax(jnp.where(gmask, gl, ninf))
    g_w = 1.0 / jnp.sum(jnp.where(gmask, jnp.exp(gl - gtop), 0.0), axis=-1, keepdims=True)
    lo = gsel * EXPERTS_PER_GROUP
    elm = jnp.where((lane >= lo) & (lane < lo + EXPERTS_PER_GROUP), el, ninf)
    t1, i1 = first_argmax(elm)
    t2, i2 = first_argmax(jnp.where(lane == i1, ninf, elm))
    e2 = jnp.exp(t2 - t1)
    gate1 = g_w * (1.0 / (1.0 + e2))
    gate2 = g_w * (e2 / (1.0 + e2))

    hot1, hot2 = lane == i1, lane == i2
    hot = (hot1 | hot2).astype(F32)
    r_i = lax.broadcasted_iota(jnp.int32, (tm, tm), 0)
    c_i = lax.broadcasted_iota(jnp.int32, (tm, tm), 1)
    before = (c_i < r_i).astype(BF16)
    rank = jnp.dot(before, hot.astype(BF16), preferred_element_type=F32) + run
    rank1 = jnp.sum(jnp.where(hot1, rank, 0.0), axis=-1, keepdims=True)
    rank2 = jnp.sum(jnp.where(hot2, rank, 0.0), axis=-1, keepdims=True)
    run = run + jnp.sum(hot, axis=0, keepdims=True)

    meta = jnp.zeros((tm, LANES), F32)
    for k, v in enumerate((i1, i2, rank1, rank2, gate1, gate2)):
        meta = jnp.where(lane == k, v, meta)
    return h, meta, run


def _dispatch_kernel(plan_ref, dest_ref, h_ref, xs_ref, zero_ref, zsem, sem):
    i = pl.program_id(0)
    tt = dest_ref.shape[2] // TOP_K
    rows = zero_ref.shape[0]
    p_rows = xs_ref.shape[0]

    def pad_block(e):
        return plan_ref[N_EXPERTS + e] - rows, plan_ref[e] > 0

    def tail_block(c):
        row0 = plan_ref[2 * N_EXPERTS - 1] + c * rows
        return row0, row0 < p_rows

    def for_zero_blocks(block, action):
        def body(c, carry):
            row0, live = block(c)

            @pl.when(live)
            def _():
                dst = xs_ref.at[pl.ds(pl.multiple_of(row0, rows), rows)]
                action(pltpu.make_async_copy(zero_ref, dst, zsem))
            return carry

        lax.fori_loop(0, N_EXPERTS, body, 0)

    @pl.when(i == 0)
    def _():
        zero_ref[...] = jnp.zeros_like(zero_ref)
        for block in (pad_block, tail_block):
            for_zero_blocks(block, lambda cp: cp.start())
        for block in (pad_block, tail_block):
            for_zero_blocks(block, lambda cp: cp.wait())

    def row_copy(tok, k):
        dst = dest_ref[0, 0, TOP_K * tok + k]
        return pltpu.make_async_copy(h_ref.at[pl.ds(tok, 1)], xs_ref.at[pl.ds(dst, 1)], sem)

    def start_rows(tok, c):
        for k in range(TOP_K):
            row_copy(tok, k).start()
        return c

    def wait_rows(tok, c):
        for k in range(TOP_K):
            row_copy(tok, k).wait()
        return c

    lax.fori_loop(0, tt, start_rows, 0, unroll=ROW_UNROLL)
    lax.fori_loop(0, tt, wait_rows, 0, unroll=ROW_UNROLL)


def _dispatch(plan, dest, h2, p_rows):
    t = h2.shape[0]
    tt = TT_ROWS
    return pl.pallas_call(
        _dispatch_kernel,
        grid_spec=pltpu.PrefetchScalarGridSpec(
            num_scalar_prefetch=1,
            grid=(t // tt,),
            in_specs=[pl.BlockSpec((1, 1, TOP_K * tt), lambda i, plan: (i, 0, 0), memory_space=pltpu.SMEM),
                      pl.BlockSpec((tt, D_PACK), lambda i, plan: (i, 0))],
            out_specs=pl.BlockSpec(memory_space=pl.ANY),
            scratch_shapes=[pltpu.VMEM((MOE_ROWS, D_PACK), jnp.uint32), pltpu.SemaphoreType.DMA(()),
                            pltpu.SemaphoreType.DMA(())]),
        out_shape=jax.ShapeDtypeStruct((p_rows, D_PACK), jnp.uint32),
        compiler_params=_cparams(1),
        name="dispatch",
    )(plan, dest, h2)


def _experts_kernel(sched_ref, nused_ref, xs_ref, wg_hbm, wu_hbm, wd_hbm, ys_ref,
                    wg_st, wu_st, wd_st, wg_bf, wu_bf, wd_bf, sem):
    i = pl.program_id(0)
    live = i < nused_ref[0]
    nblk = pl.num_programs(0)
    expert, first, nxt = (sched_ref[r * nblk + i] for r in range(3))
    pairs = ((wg_hbm, wg_st, wg_bf), (wu_hbm, wu_st, wu_bf), (wd_hbm, wd_st, wd_bf))

    def stage_copies(e):
        return [pltpu.make_async_copy(src.at[e], st, sem) for src, st, _ in pairs]

    @pl.when(i == 0)
    def _():
        for cp in stage_copies(expert):
            cp.start()

    @pl.when(live & (first == 1))
    def _():
        for cp in stage_copies(expert):
            cp.wait()
        for _, st, bf in pairs:
            chunk = CAST_ROWS * D_EXPERT // st.shape[1]

            def cast(c, carry, st=st, bf=bf, chunk=chunk):
                rows = pl.ds(pl.multiple_of(c * chunk, chunk), chunk)
                bf[rows, :] = st[rows, :].astype(BF16)
                return carry

            lax.fori_loop(0, st.shape[0] // chunk, cast, 0)

        @pl.when(nxt >= 0)
        def _():
            for cp in stage_copies(nxt):
                cp.start()

    @pl.when(live)
    def _():
        lo, hi = (v.astype(BF16) for v in _unpack_halves(xs_ref[...]))
        up = lambda w: (jnp.dot(lo, w[:D_PACK, :], preferred_element_type=F32)
                        + jnp.dot(hi, w[D_PACK:, :], preferred_element_type=F32))
        g, u = up(wg_bf), up(wu_bf)
        hid = g * (1.0 / (1.0 + jnp.exp(-g))) * u
        ys_ref[...] = _pack_halves(jnp.dot(hid.astype(BF16), wd_bf[...], preferred_element_type=F32))

    @pl.when(jnp.logical_not(live))
    def _():
        ys_ref[...] = jnp.zeros_like(ys_ref)


def _experts(sched, nused, xs, w_gate, w_up, w_down):
    p_rows = xs.shape[0]
    rows = MOE_ROWS
    live = lambda i, sched, nused: jnp.minimum(i, jnp.maximum(nused[0] - 1, 0))
    hbm = pl.BlockSpec(memory_space=pl.ANY)
    return pl.pallas_call(
        _experts_kernel,
        grid_spec=pltpu.PrefetchScalarGridSpec(
            num_scalar_prefetch=2,
            grid=(p_rows // rows,),
            in_specs=[pl.BlockSpec((rows, D_PACK), lambda i, sched, nused: (live(i, sched, nused), 0)), hbm, hbm, hbm],
            out_specs=pl.BlockSpec((rows, D_PACK), lambda i, sched, nused: (i, 0)),
            scratch_shapes=[pltpu.VMEM((D_MODEL, D_EXPERT), F32), pltpu.VMEM((D_MODEL, D_EXPERT), F32),
                            pltpu.VMEM((D_EXPERT, D_MODEL), F32),
                            pltpu.VMEM((D_MODEL, D_EXPERT), BF16), pltpu.VMEM((D_MODEL, D_EXPERT), BF16),
                            pltpu.VMEM((D_EXPERT, D_MODEL), BF16), pltpu.SemaphoreType.DMA(())]),
        out_shape=jax.ShapeDtypeStruct((p_rows, D_PACK), jnp.uint32),
        compiler_params=_cparams(1),
        name="experts",
    )(sched, nused, xs, w_gate, w_up, w_down)


def _combine_kernel(dest_ref, x_ref, meta_ref, ys_ref, o_ref, buf_ref, sem):
    tt = x_ref.shape[0]

    def row_copy(tok, k):
        src = dest_ref[0, 0, TOP_K * tok + k]
        return pltpu.make_async_copy(ys_ref.at[pl.ds(src, 1)], buf_ref.at[k, pl.ds(tok, 1)], sem)

    def start_rows(tok, c):
        for k in range(TOP_K):
            row_copy(tok, k).start()
        return c

    def wait_rows(tok, c):
        for k in range(TOP_K):
            row_copy(tok, k).wait()
        return c

    lax.fori_loop(0, tt, start_rows, 0, unroll=ROW_UNROLL)
    lax.fori_loop(0, tt, wait_rows, 0, unroll=ROW_UNROLL)
    meta = meta_ref[...]
    g1, g2 = meta[:, 4:5], meta[:, 5:6]
    (lo1, hi1), (lo2, hi2) = _unpack_halves(buf_ref[0]), _unpack_halves(buf_ref[1])
    o_ref[:, :D_PACK] = x_ref[:, :D_PACK] + (g1 * lo1 + g2 * lo2)
    o_ref[:, D_PACK:] = x_ref[:, D_PACK:] + (g1 * hi1 + g2 * hi2)


def _combine(dest, x2, meta, ys):
    t = x2.shape[0]
    tt = TT_ROWS
    return pl.pallas_call(
        _combine_kernel,
        grid=(t // tt,),
        in_specs=[pl.BlockSpec((1, 1, TOP_K * tt), lambda i: (i, 0, 0), memory_space=pltpu.SMEM),
                  pl.BlockSpec((tt, D_MODEL), lambda i: (i, 0)),
                  pl.BlockSpec((tt, LANES), lambda i: (i, 0)),
                  pl.BlockSpec(memory_space=pl.ANY)],
        out_specs=pl.BlockSpec((tt, D_MODEL), lambda i: (i, 0)),
        out_shape=jax.ShapeDtypeStruct((t, D_MODEL), F32),
        scratch_shapes=[pltpu.VMEM((TOP_K, tt, D_PACK), jnp.uint32), pltpu.SemaphoreType.DMA(())],
        compiler_params=_cparams(1),
        name="combine",
    )(dest, x2, meta, ys)


def _rope_tables(s):
    rows = s // GRID_W
    r, c = jnp.meshgrid(jnp.arange(rows, dtype=F32), jnp.arange(GRID_W, dtype=F32), indexing='ij')
    n_freq = HEAD_DIM // 4
    inv = ROPE_THETA ** (-jnp.arange(n_freq, dtype=F32) / n_freq)
    ang_r = r.reshape(-1)[:, None] * inv
    ang_c = c.reshape(-1)[:, None] * inv
    ang = jnp.concatenate([ang_r, ang_r, ang_c, ang_c], axis=-1)
    sign = np.tile(np.repeat(np.array([-1.0, 1.0], np.float32), n_freq), 2)
    return jnp.cos(ang), jnp.sin(ang) * sign


def _moe_plan(meta, cnt, t):
    rows = MOE_ROWS
    p_rows = TOP_K * t + N_EXPERTS * rows
    nblk = p_rows // rows
    counts = cnt[0, :N_EXPERTS].astype(jnp.int32)
    pcounts = (counts + rows - 1) // rows * rows
    pends = jnp.cumsum(pcounts)
    pstarts = pends - pcounts
    expert = meta[:, 0:TOP_K].astype(jnp.int32)
    rank = meta[:, TOP_K:2 * TOP_K].astype(jnp.int32)
    ids = jnp.arange(N_EXPERTS, dtype=jnp.int32)
    start = jnp.sum(jnp.where(expert[..., None] == ids, pstarts, 0), axis=-1)
    dest = (start + rank).reshape(t // TT_ROWS, 1, TOP_K * TT_ROWS)
    nused = pends[-1] // rows
    row0 = jnp.arange(nblk, dtype=jnp.int32) * rows
    blk = jnp.sum(pends[None, :] <= jnp.minimum(row0, pends[-1] - rows)[:, None], axis=1).astype(jnp.int32)
    plan = jnp.concatenate([pcounts, pends]).astype(jnp.int32)
    blocks = jnp.arange(nblk, dtype=jnp.int32)
    first = ((blocks == 0) | (blk != jnp.roll(blk, 1))) & (blocks < nused)
    later = (pcounts[None, :] > 0) & (ids[None, :] > ids[:, None])
    next_used = jnp.min(jnp.where(later, ids[None, :], N_EXPERTS), axis=1)
    next_used = jnp.where(next_used == N_EXPERTS, -1, next_used)
    nxt = jnp.sum(jnp.where(blk[:, None] == ids[None, :], next_used[None, :], 0), axis=1)
    sched = jnp.concatenate([blk, first.astype(jnp.int32), nxt]).astype(jnp.int32)
    return plan, dest, sched, nused.reshape(1).astype(jnp.int32), p_rows


def _trunk(x, p):
    b, s, _ = x.shape
    t = b * s
    assert s % (TQ_DIL * max(d for _, d in DILATED)) == 0 and s % TM_IN == 0 and t % TT_ROWS == 0
    x = x.reshape(t, D_MODEL)
    cos, sin = _rope_tables(s)
    qkv, r4, r16 = _in_proj(x, p['norm1_g'], p['w_in'], p['gain'], cos, sin, b, s)
    residue_major = {1: qkv.reshape(N_SEC, b, 1, s, SEC), 4: r4, 16: r16}
    branches = [_dilated_branch(residue_major[d], p['band'][i], b, s, d) for i, (_, d) in enumerate(DILATED)]
    ob = _gqa(qkv, b, s)
    x2, h2, meta, cnt = _out_proj([o for o, _ in branches], [l for _, l in branches], ob, x, p['on_a'], p['on_b'],
                                  p['w_out'], p['norm2_g'], p['w_r'], p['b_r'], s)
    plan, dest, sched, nused, p_rows = _moe_plan(meta, cnt, t)
    xs = _dispatch(plan, dest, h2, p_rows)
    ys = _experts(sched, nused, xs, p['w_gate'], p['w_up'], p['w_down'])
    y = _combine(dest, x2, meta, ys)
    return y.reshape(b, s, D_MODEL)


def kernel(x_prompt, x_sample, norm1_g, w_in, qn_a, kn_a, qn_b, kn_b, rel_bias, on_a, on_b, w_out, norm2_g, rg_w,
           rg_b, re_w, re_b, w_gate, w_up, w_down):
    assert norm1_g.shape[0] == 1, "one layer"
    scale = HEAD_DIM ** -0.5
    ones_a = jnp.ones((WIDTH_A,), F32)
    gain = jnp.concatenate([jnp.tile(qn_a[0], N_HEADS_A) * scale, jnp.tile(kn_a[0], N_HEADS_A), ones_a,
                            jnp.tile(qn_b[0], N_HEADS_B) * (scale * LOG2E), jnp.tile(kn_b[0], N_KV_B),
                            jnp.ones((KV_WIDTH_B,), F32)]).reshape(1, IN_WIDTH)
    pad_g = jnp.zeros((D_MODEL, LANES - N_GROUPS), F32)
    pad_e = jnp.zeros((D_MODEL, LANES - N_EXPERTS), F32)
    params = dict(
        norm1_g=norm1_g, w_in=w_in[0].astype(BF16), gain=gain,
        band=[_band_tables(rel_bias, d) for _, d in DILATED],
        on_a=on_a, on_b=on_b, w_out=w_out[0].astype(BF16), norm2_g=norm2_g,
        w_r=jnp.concatenate([rg_w[0], pad_g, re_w[0], pad_e], axis=1).astype(BF16),
        b_r=jnp.concatenate([rg_b[0], jnp.zeros((LANES - N_GROUPS,), F32), re_b[0],
                             jnp.zeros((LANES - N_EXPERTS,), F32)]).reshape(1, 2 * LANES),
        w_gate=w_gate[0], w_up=w_up[0], w_down=w_down[0])
    return _trunk(x_prompt, params), _trunk(x_sample, params)
```

```python
import functools

import numpy as np
import jax
import jax.numpy as jnp
from jax import lax
from jax.experimental import pallas as pl
from jax.experimental.pallas import tpu as pltpu

D_MODEL = 2048
HEAD_DIM = 128
N_HEADS_A = 8
N_HEADS_B = 8
N_KV_B = 2
WIDTH_A = N_HEADS_A * HEAD_DIM
WIDTH_B = N_HEADS_B * HEAD_DIM
KV_WIDTH_B = N_KV_B * HEAD_DIM
IN_WIDTH = 3 * WIDTH_A + WIDTH_B + 2 * KV_WIDTH_B
DILATED = ((128, 1), (512, 4), (2048, 16))
GRID_W = 64
ROPE_THETA = 10000.0
NUM_BUCKETS = 32
MAX_DISTANCE = 1024
N_GROUPS = 4
EXPERTS_PER_GROUP = 8
N_EXPERTS = N_GROUPS * EXPERTS_PER_GROUP
TOP_K = 2
D_EXPERT = D_MODEL // 2
D_PACK = D_MODEL // 2
EPS = 1e-6
NEG_INF = -1e30
LOG2E = float(np.log2(np.e))

LANES = 128
SEC = 4 * HEAD_DIM
N_SEC = IN_WIDTH // SEC
SEC_GROUP = 3
SEC_QA, SEC_KA, SEC_VA, SEC_QB, SEC_KVB = 0, 2, 4, 6, 8
BAND = 64
VMEM_LIMIT = 56 * 1024 * 1024

TM_IN = 512
TQ_DIL = 128
DIL_STEP = {1: (1024, 1), 4: (256, 4), 16: (128, 8)}
TQ_GQA = 256
TK_GQA = 512
UNROLL_GQA = 16
TM_OUT = 256
TT_ROWS = 256
ROW_UNROLL = 8
MOE_ROWS = 256
CAST_ROWS = 256

F32 = jnp.float32
BF16 = jnp.bfloat16


def _cparams(n_axes):
    return pltpu.CompilerParams(dimension_semantics=("arbitrary",) * n_axes, vmem_limit_bytes=VMEM_LIMIT)


def _pack_halves(a):
    n = a.shape[1] // 2
    lo = pltpu.bitcast(a[:, :n].astype(BF16).astype(F32), jnp.uint32)
    hi = pltpu.bitcast(a[:, n:].astype(BF16).astype(F32), jnp.uint32)
    return (lo >> 16) | (hi & jnp.uint32(0xFFFF0000))


def _unpack_halves(u):
    return pltpu.bitcast(u << 16, F32), pltpu.bitcast(u & jnp.uint32(0xFFFF0000), F32)


def _in_proj_kernel(x_ref, g1_ref, w_ref, gain_ref, cos_ref, sin_ref, o_ref, r4_ref, r16_ref, h_ref, y_ref, y4_ref):
    jp = pl.program_id(1)
    tm = x_ref.shape[0]
    heads = SEC // HEAD_DIM

    @pl.when(jp == 0)
    def _():
        x = x_ref[...]
        ms = jnp.mean(x * x, axis=-1, keepdims=True)
        h_ref[...] = (x * lax.rsqrt(ms + EPS) * g1_ref[...]).astype(BF16)

    lane = lax.broadcasted_iota(jnp.int32, (tm, HEAD_DIM), 1)
    first_quarter = (lane & 32) == 0

    def section(half, norm, rope, residue_major=False):
        p = jnp.dot(h_ref[...], w_ref[:, half * SEC:(half + 1) * SEC], preferred_element_type=F32)
        for hh in range(heads):
            sl = slice(hh * HEAD_DIM, (hh + 1) * HEAD_DIM)
            y = p[:, sl]
            if norm[hh]:
                ms = jnp.mean(y * y, axis=-1, keepdims=True)
                y = y * lax.rsqrt(ms + EPS)
            y = y * gain_ref[:, half * SEC + hh * HEAD_DIM:half * SEC + (hh + 1) * HEAD_DIM]
            if rope[hh]:
                partner = jnp.where(first_quarter, pltpu.roll(y, HEAD_DIM - 32, 1), pltpu.roll(y, 32, 1))
                y = y * cos_ref[...] + partner * sin_ref[...]
            o_ref[half, :, sl] = y.astype(BF16)
            if residue_major:
                slab = half * heads + hh
                y_ref[slab] = y
                for r in range(4):
                    y4 = y_ref[slab, pl.ds(r, tm // 4, stride=4), :]
                    r4_ref[half, r, :, sl] = y4.astype(BF16)
                    y4_ref[slab, pl.ds(r * (tm // 4), tm // 4), :] = y4
                for r in range(4):
                    for q in range(4):
                        y16 = y4_ref[slab, pl.ds(r * (tm // 4) + q, tm // 16, stride=4), :]
                        r16_ref[half, 4 * q + r, :, sl] = y16.astype(BF16)

    yes, no = (True,) * heads, (False,) * heads
    kv_b = (True,) * N_KV_B + (False,) * (heads - N_KV_B)

    def flags(sec):
        if sec < SEC_VA:
            return yes, no, True
        if sec < SEC_QB:
            return no, no, True
        if sec < SEC_KVB:
            return yes, yes, False
        return kv_b, kv_b, False

    for group in range(N_SEC // SEC_GROUP):
        @pl.when(jp == group)
        def _(group=group):
            for part in range(SEC_GROUP):
                section(part, *flags(group * SEC_GROUP + part))


def _in_proj(x, g1, w_in, gain, cos, sin, b, s):
    t = x.shape[0]
    tm = TM_IN
    spt = s // tm
    grp = SEC_GROUP
    n_dil = SEC_QB // grp

    def residue_major(d):
        spec = pl.BlockSpec((grp, None, d, tm // d, SEC),
                            lambda i, jp: (jnp.minimum(jp, n_dil - 1), i // spt, 0, i % spt, 0))
        return spec, jax.ShapeDtypeStruct((grp * n_dil, b, d, s // d, SEC), BF16)

    (r4_spec, r4_shape), (r16_spec, r16_shape) = residue_major(4), residue_major(16)
    return pl.pallas_call(
        _in_proj_kernel,
        grid=(t // tm, N_SEC // grp),
        in_specs=[
            pl.BlockSpec((tm, D_MODEL), lambda i, jp: (i, 0)),
            pl.BlockSpec((1, D_MODEL), lambda i, jp: (0, 0)),
            pl.BlockSpec((D_MODEL, grp * SEC), lambda i, jp: (0, jp)),
            pl.BlockSpec((1, grp * SEC), lambda i, jp: (0, jp)),
            pl.BlockSpec((tm, HEAD_DIM), lambda i, jp: (i % spt, 0)),
            pl.BlockSpec((tm, HEAD_DIM), lambda i, jp: (i % spt, 0)),
        ],
        out_specs=[pl.BlockSpec((grp, tm, SEC), lambda i, jp: (jp, i, 0)), r4_spec, r16_spec],
        out_shape=[jax.ShapeDtypeStruct((N_SEC, t, SEC), BF16), r4_shape, r16_shape],
        scratch_shapes=[pltpu.VMEM((tm, D_MODEL), BF16), pltpu.VMEM((grp * SEC // HEAD_DIM, tm, HEAD_DIM), F32),
                        pltpu.VMEM((grp * SEC // HEAD_DIM, tm, HEAD_DIM), F32)],
        compiler_params=_cparams(2),
        name="in_proj",
    )(x, g1, w_in, gain, cos, sin)


def _dilated_kernel(q_ref, kp_ref, kc_ref, kn_ref, vp_ref, vc_ref, vn_ref, a_ref, o_ref, lse_ref, *, nq, unroll):
    d, rows = q_ref.shape[0], q_ref.shape[1]
    tq = TQ_DIL
    nb = rows // tq
    lane = lax.broadcasted_iota(jnp.int32, (tq, LANES), 1)
    nt = (((1,), (1,)), ((), ()))
    first, last = pl.program_id(1) == 0, pl.program_id(1) == nq - 1

    def variant(j):
        lo, hi = (first if j == 0 else False), (last if j == nb - 1 else False)
        if lo is False and hi is False:
            return 1
        if hi is False:
            return jnp.where(lo, 0, 1)
        if lo is False:
            return jnp.where(hi, 2, 1)
        return jnp.where(lo & hi, 3, jnp.where(lo, 0, jnp.where(hi, 2, 1)))

    def residue(r, carry):
        heads = range(SEC // HEAD_DIM)
        sls = [slice(hh * HEAD_DIM, (hh + 1) * HEAD_DIM) for hh in heads]
        ks = [jnp.concatenate([kp_ref[r, :, sl], kc_ref[r, :, sl], kn_ref[r, :, sl]], axis=0) for sl in sls]
        vs = [jnp.concatenate([vp_ref[r, :, sl], vc_ref[r, :, sl], vn_ref[r, :, sl]], axis=0) for sl in sls]
        for j in range(nb):
            out_rows = pl.ds(j * tq * d + r, tq, stride=d) if d > 1 else pl.ds(j * tq, tq)
            keys = slice(j * tq, (j + 1) * tq + 2 * BAND)
            var = variant(j)
            lse_all = jnp.zeros((tq, LANES), F32)
            for hh in heads:
                q = q_ref[r, j * tq:(j + 1) * tq, sls[hh]]
                s = lax.dot_general(q, ks[hh][keys], nt, preferred_element_type=F32) + a_ref[var, hh]
                m = jnp.max(s, axis=-1, keepdims=True)
                p = jnp.exp(s - m)
                l = jnp.sum(p, axis=-1, keepdims=True)
                o = jnp.dot(p.astype(BF16), vs[hh][keys], preferred_element_type=F32)
                o_ref[hh, out_rows, :] = o / l
                lse_all = jnp.where(lane == hh, m + jnp.log(l), lse_all)
            lse_ref[out_rows, :] = lse_all
        return carry

    if d == 1:
        residue(0, 0)
    else:
        lax.fori_loop(0, d, residue, 0, unroll=unroll)


def _dilated_branch(rd, a_tab, b, s, d):
    ln = s // d
    tq = TQ_DIL
    rows, unroll = DIL_STEP[d]
    rows = min(rows, ln)
    nq = ln // rows
    edge = rows // BAND

    def cur(sec):
        return pl.BlockSpec((None, None, d, rows, SEC), lambda bi, qi, hg: (sec + hg, bi, 0, qi, 0))

    def prev(sec):
        return pl.BlockSpec((None, None, d, BAND, SEC),
                            lambda bi, qi, hg: (sec + hg, bi, 0, jnp.maximum(edge * qi - 1, 0), 0))

    def nxt(sec):
        return pl.BlockSpec((None, None, d, BAND, SEC),
                            lambda bi, qi, hg: (sec + hg, bi, 0, jnp.minimum(edge * (qi + 1), edge * nq - 1), 0))

    return pl.pallas_call(
        functools.partial(_dilated_kernel, nq=nq, unroll=unroll),
        grid=(b, nq, 2),
        in_specs=[cur(SEC_QA), prev(SEC_KA), cur(SEC_KA), nxt(SEC_KA), prev(SEC_VA), cur(SEC_VA), nxt(SEC_VA),
                  pl.BlockSpec((4, None, 4, tq, tq + 2 * BAND), lambda bi, qi, hg: (0, hg, 0, 0, 0))],
        out_specs=[pl.BlockSpec((None, 4, d * rows, HEAD_DIM), lambda bi, qi, hg: (bi, hg, qi, 0)),
                   pl.BlockSpec((None, None, d * rows, LANES), lambda bi, qi, hg: (hg, bi, qi, 0))],
        out_shape=[jax.ShapeDtypeStruct((b, N_HEADS_A, s, HEAD_DIM), F32),
                   jax.ShapeDtypeStruct((2, b, s, LANES), F32)],
        compiler_params=_cparams(3),
        name=f"dilated_d{d}",
    )(rd, rd, rd, rd, rd, rd, rd, a_tab)


def _t5_buckets(rel):
    nb = NUM_BUCKETS // 2
    max_exact = nb // 2
    n = np.abs(rel)
    large = max_exact + (np.log(np.maximum(n, 1) / max_exact) / np.log(MAX_DISTANCE / max_exact)
                         * (nb - max_exact)).astype(np.int32)
    large = np.minimum(large, nb - 1)
    return (rel > 0).astype(np.int32) * nb + np.where(n < max_exact, n, large).astype(np.int32)


def _band_tables(rel_bias, d):
    tq = TQ_DIL
    width = tq + 2 * BAND
    rel = np.arange(width + tq - 1) - (tq - 1) - BAND
    bucket = _t5_buckets(d * np.clip(rel, -BAND, BAND))
    diag = jnp.where(np.abs(rel) <= BAND, rel_bias.astype(F32)[bucket].T, NEG_INF)
    n = diag.shape[1]
    flat = jnp.tile(diag, (1, tq))[:, tq - 1:tq - 1 + tq * (n - 1)]
    table = flat.reshape(N_HEADS_A, tq, n - 1)[:, :, :width]
    c = np.arange(width)[None, None, :]
    prev_ok, next_ok = c >= BAND, c < BAND + tq
    tabs = [jnp.where(keep, table, NEG_INF) for keep in (prev_ok, c >= 0, next_ok, prev_ok & next_ok)]
    return jnp.stack(tabs).reshape(4, 2, 4, tq, width)


def _gqa_kernel(q_ref, k_ref, v_ref, o_ref, vx_ref, *, tk, unroll):
    tq = q_ref.shape[0]
    nk = k_ref.shape[0] // tk
    rep = SEC // HEAD_DIM

    @pl.when(pl.program_id(2) == 0)
    def _():
        vx_ref[:, :HEAD_DIM] = v_ref[...]
        vx_ref[:, HEAD_DIM:] = jnp.ones((vx_ref.shape[0], HEAD_DIM), BF16)

    q = jnp.concatenate([q_ref[:, hh * HEAD_DIM:(hh + 1) * HEAD_DIM] for hh in range(rep)], axis=0)
    nt = (((1,), (1,)), ((), ()))

    def body(c, carry):
        m, l, acc = carry
        start = pl.multiple_of(c * tk, tk)
        s = lax.dot_general(q, k_ref[pl.ds(start, tk), :], nt, preferred_element_type=F32)
        m_new = jnp.maximum(m, jnp.max(s, axis=-1, keepdims=True))
        alpha = jnp.exp2(m - m_new)
        p = jnp.exp2((s - m_new).astype(BF16))
        pv = jnp.dot(p, vx_ref[pl.ds(start, tk), :], preferred_element_type=F32)
        l = alpha * l + pv[:, HEAD_DIM:HEAD_DIM + 1]
        acc = alpha * acc + pv[:, :HEAD_DIM]
        return m_new, l, acc

    init = (jnp.full((rep * tq, 1), NEG_INF, F32), jnp.zeros((rep * tq, 1), F32),
            jnp.zeros((rep * tq, HEAD_DIM), F32))
    _, l, acc = lax.fori_loop(0, nk, body, init, unroll=unroll)
    o = acc / l
    for hh in range(rep):
        o_ref[:, hh * HEAD_DIM:(hh + 1) * HEAD_DIM] = o[hh * tq:(hh + 1) * tq]


def _gqa(qkv, b, s):
    tq = TQ_GQA
    tk = min(TK_GQA, s)
    view = qkv.reshape(N_SEC, b, s, SEC)
    o = pl.pallas_call(
        functools.partial(_gqa_kernel, tk=tk, unroll=min(UNROLL_GQA, s // tk)),
        grid=(b, N_KV_B, s // tq),
        in_specs=[pl.BlockSpec((None, None, tq, SEC), lambda bi, g, qi: (SEC_QB + g, bi, qi, 0)),
                  pl.BlockSpec((None, None, s, HEAD_DIM), lambda bi, g, qi: (SEC_KVB, bi, 0, g)),
                  pl.BlockSpec((None, None, s, HEAD_DIM), lambda bi, g, qi: (SEC_KVB, bi, 0, N_KV_B + g))],
        out_specs=pl.BlockSpec((None, tq, SEC), lambda bi, g, qi: (bi, qi, g)),
        out_shape=jax.ShapeDtypeStruct((b, s, WIDTH_B), F32),
        scratch_shapes=[pltpu.VMEM((s, 2 * HEAD_DIM), BF16)],
        compiler_params=_cparams(3),
        name="gqa",
    )(view, view, view)
    return o.reshape(b * s, WIDTH_B)


def _out_proj_kernel(o1_ref, o2_ref, o3_ref, l1_ref, l2_ref, l3_ref, ob_ref, x_ref, ga_ref, gb_ref, w_ref,
                     g2_ref, wr_ref, br_ref, run0_ref, y_ref, h_ref, meta_ref, cnt_ref, run_ref):
    @pl.when(pl.program_id(0) == 0)
    def _():
        run_ref[...] = run0_ref[...]

    lses = (l1_ref[...], l2_ref[...], l3_ref[...])
    m = jnp.maximum(jnp.maximum(lses[0], lses[1]), lses[2])
    es = [jnp.exp(v - m) for v in lses]
    tot = es[0] + es[1] + es[2]
    ws = [e / tot for e in es]
    heads = []
    for h in range(N_HEADS_A):
        hg, hl = divmod(h, 4)
        acc = None
        for w, o_ref in zip(ws, (o1_ref, o2_ref, o3_ref)):
            term = w[hg][:, hl:hl + 1] * o_ref[h]
            acc = term if acc is None else acc + term
        heads.append(acc)
    oa = jnp.concatenate(heads, axis=1)
    oa = oa * lax.rsqrt(jnp.mean(oa * oa, axis=-1, keepdims=True) + EPS) * ga_ref[...]
    ob = ob_ref[...]
    ob = ob * lax.rsqrt(jnp.mean(ob * ob, axis=-1, keepdims=True) + EPS) * gb_ref[...]
    y = (jnp.dot(oa.astype(BF16), w_ref[:WIDTH_A, :], preferred_element_type=F32)
         + jnp.dot(ob.astype(BF16), w_ref[WIDTH_A:, :], preferred_element_type=F32))
    x2 = x_ref[...] + y
    y_ref[...] = x2
    h, meta, run = _route_tile(x2, g2_ref[...], wr_ref[...], br_ref[...], run_ref[...])
    h_ref[...] = _pack_halves(h)
    meta_ref[...] = meta
    run_ref[...] = run
    cnt_ref[...] = run


def _out_proj(os_, lses, ob, x, on_a, on_b, w_out, g2, w_r, b_r, run0, s):
    t = x.shape[0]
    tm = TM_OUT
    spt = s // tm
    row = lambda width: pl.BlockSpec((tm, width), lambda i: (i, 0))
    o_spec = pl.BlockSpec((None, N_HEADS_A, tm, HEAD_DIM), lambda i: (i // spt, 0, i % spt, 0))
    lse_spec = pl.BlockSpec((2, None, tm, LANES), lambda i: (0, i // spt, i % spt, 0))
    const = lambda shape: pl.BlockSpec(shape, lambda i: (0, 0))
    return pl.pallas_call(
        _out_proj_kernel,
        grid=(t // tm,),
        in_specs=[o_spec, o_spec, o_spec, lse_spec, lse_spec, lse_spec, row(WIDTH_B),
                  row(D_MODEL), const((1, WIDTH_A)), const((1, WIDTH_B)), const((WIDTH_A + WIDTH_B, D_MODEL)),
                  const((1, D_MODEL)), const((D_MODEL, 2 * LANES)), const((1, 2 * LANES)), const((1, LANES))],
        out_specs=[row(D_MODEL), row(D_PACK), row(LANES), const((1, LANES))],
        out_shape=[jax.ShapeDtypeStruct((t, D_MODEL), F32), jax.ShapeDtypeStruct((t, D_PACK), jnp.uint32),
                   jax.ShapeDtypeStruct((t, LANES), F32), jax.ShapeDtypeStruct((1, LANES), F32)],
        scratch_shapes=[pltpu.VMEM((1, LANES), F32)],
        compiler_params=_cparams(1),
        name="out_proj",
    )(*os_, *lses, ob, x, on_a, on_b, w_out, g2, w_r, b_r, run0)


def _route_tile(x, g2, w_r, b_r, run):
    tm = x.shape[0]
    h = x * lax.rsqrt(jnp.mean(x * x, axis=-1, keepdims=True) + EPS) * g2
    logits = jnp.dot(h.astype(BF16), w_r, preferred_element_type=F32) + b_r
    gl, el = logits[:, :LANES], logits[:, LANES:]
    lane = lax.broadcasted_iota(jnp.int32, (tm, LANES), 1).astype(F32)
    ninf = jnp.float32(-jnp.inf)
    big = jnp.float32(LANES)

    def first_argmax(v):
        top = jnp.max(v, axis=-1, keepdims=True)
        return top, jnp.min(jnp.where(v == top, lane, big), axis=-1, keepdims=True)

    gmask = lane < N_GROUPS
    gtop, gsel = first_argmax(jnp.where(gmask, gl, ninf))
    g_w = 1.0 / jnp.sum(jnp.where(gmask, jnp.exp(gl - gtop), 0.0), axis=-1, keepdims=True)
    lo = gsel * EXPERTS_PER_GROUP
    elm = jnp.where((lane >= lo) & (lane < lo + EXPERTS_PER_GROUP), el, ninf)
    t1, i1 = first_argmax(elm)
    t2, i2 = first_argmax(jnp.where(lane == i1, ninf, elm))
    e2 = jnp.exp(t2 - t1)
    gate1 = g_w * (1.0 / (1.0 + e2))
    gate2 = g_w * (e2 / (1.0 + e2))

    hot1, hot2 = lane == i1, lane == i2
    hot = (hot1 | hot2).astype(F32)
    r_i = lax.broadcasted_iota(jnp.int32, (tm, tm), 0)
    c_i = lax.broadcasted_iota(jnp.int32, (tm, tm), 1)
    before = (c_i < r_i).astype(BF16)
    rank = jnp.dot(before, hot.astype(BF16), preferred_element_type=F32) + run
    rank1 = jnp.sum(jnp.where(hot1, rank, 0.0), axis=-1, keepdims=True)
    rank2 = jnp.sum(jnp.where(hot2, rank, 0.0), axis=-1, keepdims=True)
    run = run + jnp.sum(hot, axis=0, keepdims=True)

    meta = jnp.zeros((tm, LANES), F32)
    for k, v in enumerate((i1, i2, rank1, rank2, gate1, gate2)):
        meta = jnp.where(lane == k, v, meta)
    return h, meta, run


def _dispatch_kernel(plan_ref, dest_ref, ha_ref, hb_ref, xs_ref, zero_ref, zsem, sem, *, steps_a):
    i = pl.program_id(0)
    tt = dest_ref.shape[2] // TOP_K
    rows = zero_ref.shape[0]
    p_rows = xs_ref.shape[0]

    def pad_block(e):
        return plan_ref[N_EXPERTS + e] - rows, plan_ref[e] > 0

    def tail_block(c):
        row0 = plan_ref[2 * N_EXPERTS - 1] + c * rows
        return row0, row0 < p_rows

    def for_zero_blocks(block, action):
        def body(c, carry):
            row0, live = block(c)

            @pl.when(live)
            def _():
                dst = xs_ref.at[pl.ds(pl.multiple_of(row0, rows), rows)]
                action(pltpu.make_async_copy(zero_ref, dst, zsem))
            return carry

        lax.fori_loop(0, N_EXPERTS, body, 0)

    @pl.when(i == 0)
    def _():
        zero_ref[...] = jnp.zeros_like(zero_ref)
        for block in (pad_block, tail_block):
            for_zero_blocks(block, lambda cp: cp.start())
        for block in (pad_block, tail_block):
            for_zero_blocks(block, lambda cp: cp.wait())

    def scatter_rows(h_ref):
        def row_copy(tok, k):
            dst = dest_ref[0, 0, TOP_K * tok + k]
            return pltpu.make_async_copy(h_ref.at[pl.ds(tok, 1)], xs_ref.at[pl.ds(dst, 1)], sem)

        def start_rows(tok, c):
            for k in range(TOP_K):
                row_copy(tok, k).start()
            return c

        def wait_rows(tok, c):
            for k in range(TOP_K):
                row_copy(tok, k).wait()
            return c

        lax.fori_loop(0, tt, start_rows, 0, unroll=ROW_UNROLL)
        lax.fori_loop(0, tt, wait_rows, 0, unroll=ROW_UNROLL)

    @pl.when(i < steps_a)
    def _():
        scatter_rows(ha_ref)

    @pl.when(i >= steps_a)
    def _():
        scatter_rows(hb_ref)


def _dispatch(plan, dest, h_a, h_b, p_rows):
    tt = TT_ROWS
    steps_a, steps_b = h_a.shape[0] // tt, h_b.shape[0] // tt
    return pl.pallas_call(
        functools.partial(_dispatch_kernel, steps_a=steps_a),
        grid_spec=pltpu.PrefetchScalarGridSpec(
            num_scalar_prefetch=1,
            grid=(steps_a + steps_b,),
            in_specs=[pl.BlockSpec((1, 1, TOP_K * tt), lambda i, plan: (i, 0, 0), memory_space=pltpu.SMEM),
                      pl.BlockSpec((tt, D_PACK), lambda i, plan: (jnp.minimum(i, steps_a - 1), 0)),
                      pl.BlockSpec((tt, D_PACK), lambda i, plan: (jnp.maximum(i - steps_a, 0), 0))],
            out_specs=pl.BlockSpec(memory_space=pl.ANY),
            scratch_shapes=[pltpu.VMEM((MOE_ROWS, D_PACK), jnp.uint32), pltpu.SemaphoreType.DMA(()),
                            pltpu.SemaphoreType.DMA(())]),
        out_shape=jax.ShapeDtypeStruct((p_rows, D_PACK), jnp.uint32),
        compiler_params=_cparams(1),
        name="dispatch",
    )(plan, dest, h_a, h_b)


def _experts_kernel(sched_ref, nused_ref, xs_ref, wg_hbm, wu_hbm, wd_hbm, ys_ref,
                    wg_st, wu_st, wd_st, wg_bf, wu_bf, wd_bf, sem):
    i = pl.program_id(0)
    live = i < nused_ref[0]
    nblk = pl.num_programs(0)
    expert, first, nxt = (sched_ref[r * nblk + i] for r in range(3))
    pairs = ((wg_hbm, wg_st, wg_bf), (wu_hbm, wu_st, wu_bf), (wd_hbm, wd_st, wd_bf))

    def stage_copies(e):
        return [pltpu.make_async_copy(src.at[e], st, sem) for src, st, _ in pairs]

    @pl.when(i == 0)
    def _():
        for cp in stage_copies(expert):
            cp.start()

    @pl.when(live & (first == 1))
    def _():
        for cp in stage_copies(expert):
            cp.wait()
        for _, st, bf in pairs:
            chunk = CAST_ROWS * D_EXPERT // st.shape[1]

            def cast(c, carry, st=st, bf=bf, chunk=chunk):
                rows = pl.ds(pl.multiple_of(c * chunk, chunk), chunk)
                bf[rows, :] = st[rows, :].astype(BF16)
                return carry

            lax.fori_loop(0, st.shape[0] // chunk, cast, 0)

        @pl.when(nxt >= 0)
        def _():
            for cp in stage_copies(nxt):
                cp.start()

    @pl.when(live)
    def _():
        lo, hi = (v.astype(BF16) for v in _unpack_halves(xs_ref[...]))
        up = lambda w: (jnp.dot(lo, w[:D_PACK, :], preferred_element_type=F32)
                        + jnp.dot(hi, w[D_PACK:, :], preferred_element_type=F32))
        g, u = up(wg_bf), up(wu_bf)
        hid = g * (1.0 / (1.0 + jnp.exp(-g))) * u
        ys_ref[...] = _pack_halves(jnp.dot(hid.astype(BF16), wd_bf[...], preferred_element_type=F32))

    @pl.when(jnp.logical_not(live))
    def _():
        ys_ref[...] = jnp.zeros_like(ys_ref)


def _experts(sched, nused, xs, w_gate, w_up, w_down):
    p_rows = xs.shape[0]
    rows = MOE_ROWS
    live = lambda i, sched, nused: jnp.minimum(i, jnp.maximum(nused[0] - 1, 0))
    hbm = pl.BlockSpec(memory_space=pl.ANY)
    return pl.pallas_call(
        _experts_kernel,
        grid_spec=pltpu.PrefetchScalarGridSpec(
            num_scalar_prefetch=2,
            grid=(p_rows // rows,),
            in_specs=[pl.BlockSpec((rows, D_PACK), lambda i, sched, nused: (live(i, sched, nused), 0)), hbm, hbm, hbm],
            out_specs=pl.BlockSpec((rows, D_PACK), lambda i, sched, nused: (i, 0)),
            scratch_shapes=[pltpu.VMEM((D_MODEL, D_EXPERT), F32), pltpu.VMEM((D_MODEL, D_EXPERT), F32),
                            pltpu.VMEM((D_EXPERT, D_MODEL), F32),
                            pltpu.VMEM((D_MODEL, D_EXPERT), BF16), pltpu.VMEM((D_MODEL, D_EXPERT), BF16),
                            pltpu.VMEM((D_EXPERT, D_MODEL), BF16), pltpu.SemaphoreType.DMA(())]),
        out_shape=jax.ShapeDtypeStruct((p_rows, D_PACK), jnp.uint32),
        compiler_params=_cparams(1),
        name="experts",
    )(sched, nused, xs, w_gate, w_up, w_down)


def _combine_kernel(dest_ref, x_ref, meta_ref, ys_ref, o_ref, buf_ref, sem):
    tt = x_ref.shape[0]

    def row_copy(tok, k):
        src = dest_ref[0, 0, TOP_K * tok + k]
        return pltpu.make_async_copy(ys_ref.at[pl.ds(src, 1)], buf_ref.at[k, pl.ds(tok, 1)], sem)

    def start_rows(tok, c):
        for k in range(TOP_K):
            row_copy(tok, k).start()
        return c

    def wait_rows(tok, c):
        for k in range(TOP_K):
            row_copy(tok, k).wait()
        return c

    lax.fori_loop(0, tt, start_rows, 0, unroll=ROW_UNROLL)
    lax.fori_loop(0, tt, wait_rows, 0, unroll=ROW_UNROLL)
    meta = meta_ref[...]
    g1, g2 = meta[:, 4:5], meta[:, 5:6]
    (lo1, hi1), (lo2, hi2) = _unpack_halves(buf_ref[0]), _unpack_halves(buf_ref[1])
    o_ref[:, :D_PACK] = x_ref[:, :D_PACK] + (g1 * lo1 + g2 * lo2)
    o_ref[:, D_PACK:] = x_ref[:, D_PACK:] + (g1 * hi1 + g2 * hi2)


def _combine(dest, x2, meta, ys):
    t = x2.shape[0]
    tt = TT_ROWS
    return pl.pallas_call(
        _combine_kernel,
        grid=(t // tt,),
        in_specs=[pl.BlockSpec((1, 1, TOP_K * tt), lambda i: (i, 0, 0), memory_space=pltpu.SMEM),
                  pl.BlockSpec((tt, D_MODEL), lambda i: (i, 0)),
                  pl.BlockSpec((tt, LANES), lambda i: (i, 0)),
                  pl.BlockSpec(memory_space=pl.ANY)],
        out_specs=pl.BlockSpec((tt, D_MODEL), lambda i: (i, 0)),
        out_shape=jax.ShapeDtypeStruct((t, D_MODEL), F32),
        scratch_shapes=[pltpu.VMEM((TOP_K, tt, D_PACK), jnp.uint32), pltpu.SemaphoreType.DMA(())],
        compiler_params=_cparams(1),
        name="combine",
    )(dest, x2, meta, ys)


def _rope_tables(s):
    rows = s // GRID_W
    r, c = jnp.meshgrid(jnp.arange(rows, dtype=F32), jnp.arange(GRID_W, dtype=F32), indexing='ij')
    n_freq = HEAD_DIM // 4
    inv = ROPE_THETA ** (-jnp.arange(n_freq, dtype=F32) / n_freq)
    ang_r = r.reshape(-1)[:, None] * inv
    ang_c = c.reshape(-1)[:, None] * inv
    ang = jnp.concatenate([ang_r, ang_r, ang_c, ang_c], axis=-1)
    sign = np.tile(np.repeat(np.array([-1.0, 1.0], np.float32), n_freq), 2)
    return jnp.cos(ang), jnp.sin(ang) * sign


def _moe_plan(metas, cnt):
    rows = MOE_ROWS
    t = sum(m.shape[0] for m in metas)
    p_rows = TOP_K * t + N_EXPERTS * rows
    nblk = p_rows // rows
    counts = cnt[0, :N_EXPERTS].astype(jnp.int32)
    pcounts = (counts + rows - 1) // rows * rows
    pends = jnp.cumsum(pcounts)
    pstarts = pends - pcounts
    ids = jnp.arange(N_EXPERTS, dtype=jnp.int32)

    def slots(meta):
        expert = meta[:, 0:TOP_K].astype(jnp.int32)
        rank = meta[:, TOP_K:2 * TOP_K].astype(jnp.int32)
        start = jnp.sum(jnp.where(expert[..., None] == ids, pstarts, 0), axis=-1)
        return (start + rank).reshape(meta.shape[0] // TT_ROWS, 1, TOP_K * TT_ROWS)

    dest = [slots(m) for m in metas]
    nused = pends[-1] // rows
    row0 = jnp.arange(nblk, dtype=jnp.int32) * rows
    blk = jnp.sum(pends[None, :] <= jnp.minimum(row0, pends[-1] - rows)[:, None], axis=1).astype(jnp.int32)
    plan = jnp.concatenate([pcounts, pends]).astype(jnp.int32)
    blocks = jnp.arange(nblk, dtype=jnp.int32)
    first = ((blocks == 0) | (blk != jnp.roll(blk, 1))) & (blocks < nused)
    later = (pcounts[None, :] > 0) & (ids[None, :] > ids[:, None])
    next_used = jnp.min(jnp.where(later, ids[None, :], N_EXPERTS), axis=1)
    next_used = jnp.where(next_used == N_EXPERTS, -1, next_used)
    nxt = jnp.sum(jnp.where(blk[:, None] == ids[None, :], next_used[None, :], 0), axis=1)
    sched = jnp.concatenate([blk, first.astype(jnp.int32), nxt]).astype(jnp.int32)
    return plan, dest, sched, nused.reshape(1).astype(jnp.int32), p_rows


def _attend_and_route(x, p, run0):
    b, s, _ = x.shape
    t = b * s
    assert s % (TQ_DIL * max(d for _, d in DILATED)) == 0 and s % TM_IN == 0 and t % TT_ROWS == 0
    x = x.reshape(t, D_MODEL)
    cos, sin = _rope_tables(s)
    qkv, r4, r16 = _in_proj(x, p['norm1_g'], p['w_in'], p['gain'], cos, sin, b, s)
    residue_major = {1: qkv.reshape(N_SEC, b, 1, s, SEC), 4: r4, 16: r16}
    branches = [_dilated_branch(residue_major[d], p['band'][i], b, s, d) for i, (_, d) in enumerate(DILATED)]
    ob = _gqa(qkv, b, s)
    return _out_proj([o for o, _ in branches], [l for _, l in branches], ob, x, p['on_a'], p['on_b'],
                     p['w_out'], p['norm2_g'], p['w_r'], p['b_r'], run0, s)


def _layer(x_a, x_b, p):
    x2_a, h_a, meta_a, cnt_a = _attend_and_route(x_a, p, jnp.zeros((1, LANES), F32))
    x2_b, h_b, meta_b, cnt = _attend_and_route(x_b, p, cnt_a)
    plan, (dest_a, dest_b), sched, nused, p_rows = _moe_plan([meta_a, meta_b], cnt)
    xs = _dispatch(plan, jnp.concatenate([dest_a, dest_b]), h_a, h_b, p_rows)
    ys = _experts(sched, nused, xs, p['w_gate'], p['w_up'], p['w_down'])
    return (_combine(dest_a, x2_a, meta_a, ys).reshape(x_a.shape),
            _combine(dest_b, x2_b, meta_b, ys).reshape(x_b.shape))


def kernel(x_prompt, x_sample, norm1_g, w_in, qn_a, kn_a, qn_b, kn_b, rel_bias, on_a, on_b, w_out, norm2_g, rg_w,
           rg_b, re_w, re_b, w_gate, w_up, w_down):
    assert norm1_g.shape[0] == 1, "one layer"
    scale = HEAD_DIM ** -0.5
    ones_a = jnp.ones((WIDTH_A,), F32)
    gain = jnp.concatenate([jnp.tile(qn_a[0], N_HEADS_A) * scale, jnp.tile(kn_a[0], N_HEADS_A), ones_a,
                            jnp.tile(qn_b[0], N_HEADS_B) * (scale * LOG2E), jnp.tile(kn_b[0], N_KV_B),
                            jnp.ones((KV_WIDTH_B,), F32)]).reshape(1, IN_WIDTH)
    pad_g = jnp.zeros((D_MODEL, LANES - N_GROUPS), F32)
    pad_e = jnp.zeros((D_MODEL, LANES - N_EXPERTS), F32)
    params = dict(
        norm1_g=norm1_g, w_in=w_in[0].astype(BF16), gain=gain,
        band=[_band_tables(rel_bias, d) for _, d in DILATED],
        on_a=on_a, on_b=on_b, w_out=w_out[0].astype(BF16), norm2_g=norm2_g,
        w_r=jnp.concatenate([rg_w[0], pad_g, re_w[0], pad_e], axis=1).astype(BF16),
        b_r=jnp.concatenate([rg_b[0], jnp.zeros((LANES - N_GROUPS,), F32), re_b[0],
                             jnp.zeros((LANES - N_EXPERTS,), F32)]).reshape(1, 2 * LANES),
        w_gate=w_gate[0], w_up=w_up[0], w_down=w_down[0])
    return _layer(x_prompt, x_sample, params)
```

```python
import functools

import numpy as np
import jax
import jax.numpy as jnp
from jax import lax
from jax.experimental import pallas as pl
from jax.experimental.pallas import tpu as pltpu

D_MODEL = 2048
HEAD_DIM = 128
N_HEADS_A = 8
N_HEADS_B = 8
N_KV_B = 2
WIDTH_A = N_HEADS_A * HEAD_DIM
WIDTH_B = N_HEADS_B * HEAD_DIM
KV_WIDTH_B = N_KV_B * HEAD_DIM
IN_WIDTH = 3 * WIDTH_A + WIDTH_B + 2 * KV_WIDTH_B
DILATED = ((128, 1), (512, 4), (2048, 16))
GRID_W = 64
ROPE_THETA = 10000.0
NUM_BUCKETS = 32
MAX_DISTANCE = 1024
N_GROUPS = 4
EXPERTS_PER_GROUP = 8
N_EXPERTS = N_GROUPS * EXPERTS_PER_GROUP
TOP_K = 2
D_EXPERT = D_MODEL // 2
D_PACK = D_MODEL // 2
EPS = 1e-6
NEG_INF = -1e30
LOG2E = float(np.log2(np.e))

LANES = 128
SEC = 4 * HEAD_DIM
N_SEC = IN_WIDTH // SEC
SEC_GROUP = 3
SEC_QA, SEC_KA, SEC_VA, SEC_QB, SEC_KVB = 0, 2, 4, 6, 8
BAND = 64
VMEM_LIMIT = 56 * 1024 * 1024

TM_IN = 512
TQ_DIL = 128
DIL_STEP = {1: (1024, 1), 4: (256, 4), 16: (128, 8)}
TQ_GQA = 256
TK_GQA = 512
UNROLL_GQA = 16
TM_OUT = 256
TT_ROWS = 256
ROW_UNROLL = 8
MOE_ROWS = 256
CAST_ROWS = 256

F32 = jnp.float32
BF16 = jnp.bfloat16


def _cparams(n_axes):
    return pltpu.CompilerParams(dimension_semantics=("arbitrary",) * n_axes, vmem_limit_bytes=VMEM_LIMIT)


def _pack_halves(a):
    n = a.shape[1] // 2
    lo = pltpu.bitcast(a[:, :n].astype(BF16).astype(F32), jnp.uint32)
    hi = pltpu.bitcast(a[:, n:].astype(BF16).astype(F32), jnp.uint32)
    return (lo >> 16) | (hi & jnp.uint32(0xFFFF0000))


def _unpack_halves(u):
    return pltpu.bitcast(u << 16, F32), pltpu.bitcast(u & jnp.uint32(0xFFFF0000), F32)


def _in_proj_kernel(x_ref, g1_ref, w_ref, gain_ref, cos_ref, sin_ref, o_ref, r4_ref, r16_ref, h_ref, y_ref, y4_ref):
    jp = pl.program_id(1)
    tm = x_ref.shape[0]
    heads = SEC // HEAD_DIM

    @pl.when(jp == 0)
    def _():
        x = x_ref[...]
        ms = jnp.mean(x * x, axis=-1, keepdims=True)
        h_ref[...] = (x * lax.rsqrt(ms + EPS) * g1_ref[...]).astype(BF16)

    lane = lax.broadcasted_iota(jnp.int32, (tm, HEAD_DIM), 1)
    first_quarter = (lane & 32) == 0

    def section(half, norm, rope, residue_major=False):
        p = jnp.dot(h_ref[...], w_ref[:, half * SEC:(half + 1) * SEC], preferred_element_type=F32)
        for hh in range(heads):
            sl = slice(hh * HEAD_DIM, (hh + 1) * HEAD_DIM)
            y = p[:, sl]
            if norm[hh]:
                ms = jnp.mean(y * y, axis=-1, keepdims=True)
                y = y * lax.rsqrt(ms + EPS)
            y = y * gain_ref[:, half * SEC + hh * HEAD_DIM:half * SEC + (hh + 1) * HEAD_DIM]
            if rope[hh]:
                partner = jnp.where(first_quarter, pltpu.roll(y, HEAD_DIM - 32, 1), pltpu.roll(y, 32, 1))
                y = y * cos_ref[...] + partner * sin_ref[...]
            o_ref[half, :, sl] = y.astype(BF16)
            if residue_major:
                slab = half * heads + hh
                y_ref[slab] = y
                for r in range(4):
                    y4 = y_ref[slab, pl.ds(r, tm // 4, stride=4), :]
                    r4_ref[half, r, :, sl] = y4.astype(BF16)
                    y4_ref[slab, pl.ds(r * (tm // 4), tm // 4), :] = y4
                for r in range(4):
                    for q in range(4):
                        y16 = y4_ref[slab, pl.ds(r * (tm // 4) + q, tm // 16, stride=4), :]
                        r16_ref[half, 4 * q + r, :, sl] = y16.astype(BF16)

    yes, no = (True,) * heads, (False,) * heads
    kv_b = (True,) * N_KV_B + (False,) * (heads - N_KV_B)

    def flags(sec):
        if sec < SEC_VA:
            return yes, no, True
        if sec < SEC_QB:
            return no, no, True
        if sec < SEC_KVB:
            return yes, yes, False
        return kv_b, kv_b, False

    for group in range(N_SEC // SEC_GROUP):
        @pl.when(jp == group)
        def _(group=group):
            for part in range(SEC_GROUP):
                section(part, *flags(group * SEC_GROUP + part))


def _in_proj(x, g1, w_in, gain, cos, sin, b, s):
    t = x.shape[0]
    tm = TM_IN
    spt = s // tm
    grp = SEC_GROUP
    n_dil = SEC_QB // grp

    def residue_major(d):
        spec = pl.BlockSpec((grp, None, d, tm // d, SEC),
                            lambda i, jp: (jnp.minimum(jp, n_dil - 1), i // spt, 0, i % spt, 0))
        return spec, jax.ShapeDtypeStruct((grp * n_dil, b, d, s // d, SEC), BF16)

    (r4_spec, r4_shape), (r16_spec, r16_shape) = residue_major(4), residue_major(16)
    return pl.pallas_call(
        _in_proj_kernel,
        grid=(t // tm, N_SEC // grp),
        in_specs=[
            pl.BlockSpec((tm, D_MODEL), lambda i, jp: (i, 0)),
            pl.BlockSpec((1, D_MODEL), lambda i, jp: (0, 0)),
            pl.BlockSpec((D_MODEL, grp * SEC), lambda i, jp: (0, jp)),
            pl.BlockSpec((1, grp * SEC), lambda i, jp: (0, jp)),
            pl.BlockSpec((tm, HEAD_DIM), lambda i, jp: (i % spt, 0)),
            pl.BlockSpec((tm, HEAD_DIM), lambda i, jp: (i % spt, 0)),
        ],
        out_specs=[pl.BlockSpec((grp, tm, SEC), lambda i, jp: (jp, i, 0)), r4_spec, r16_spec],
        out_shape=[jax.ShapeDtypeStruct((N_SEC, t, SEC), BF16), r4_shape, r16_shape],
        scratch_shapes=[pltpu.VMEM((tm, D_MODEL), BF16), pltpu.VMEM((grp * SEC // HEAD_DIM, tm, HEAD_DIM), F32),
                        pltpu.VMEM((grp * SEC // HEAD_DIM, tm, HEAD_DIM), F32)],
        compiler_params=_cparams(2),
        name="in_proj",
    )(x, g1, w_in, gain, cos, sin)


def _dilated_kernel(q_ref, kp_ref, kc_ref, kn_ref, vp_ref, vc_ref, vn_ref, a_ref, o_ref, lse_ref, *, nq, unroll):
    d, rows = q_ref.shape[0], q_ref.shape[1]
    tq = TQ_DIL
    nb = rows // tq
    lane = lax.broadcasted_iota(jnp.int32, (tq, LANES), 1)
    nt = (((1,), (1,)), ((), ()))
    first, last = pl.program_id(1) == 0, pl.program_id(1) == nq - 1

    def variant(j):
        lo, hi = (first if j == 0 else False), (last if j == nb - 1 else False)
        if lo is False and hi is False:
            return 1
        if hi is False:
            return jnp.where(lo, 0, 1)
        if lo is False:
            return jnp.where(hi, 2, 1)
        return jnp.where(lo & hi, 3, jnp.where(lo, 0, jnp.where(hi, 2, 1)))

    def residue(r, carry):
        heads = range(SEC // HEAD_DIM)
        sls = [slice(hh * HEAD_DIM, (hh + 1) * HEAD_DIM) for hh in heads]
        ks = [jnp.concatenate([kp_ref[r, :, sl], kc_ref[r, :, sl], kn_ref[r, :, sl]], axis=0) for sl in sls]
        vs = [jnp.concatenate([vp_ref[r, :, sl], vc_ref[r, :, sl], vn_ref[r, :, sl]], axis=0) for sl in sls]
        for j in range(nb):
            out_rows = pl.ds(j * tq * d + r, tq, stride=d) if d > 1 else pl.ds(j * tq, tq)
            keys = slice(j * tq, (j + 1) * tq + 2 * BAND)
            var = variant(j)
            lse_all = jnp.zeros((tq, LANES), F32)
            for hh in heads:
                q = q_ref[r, j * tq:(j + 1) * tq, sls[hh]]
                s = lax.dot_general(q, ks[hh][keys], nt, preferred_element_type=F32) + a_ref[var, hh]
                m = jnp.max(s, axis=-1, keepdims=True)
                p = jnp.exp(s - m)
                l = jnp.sum(p, axis=-1, keepdims=True)
                o = jnp.dot(p.astype(BF16), vs[hh][keys], preferred_element_type=F32)
                o_ref[hh, out_rows, :] = o / l
                lse_all = jnp.where(lane == hh, m + jnp.log(l), lse_all)
            lse_ref[out_rows, :] = lse_all
        return carry

    if d == 1:
        residue(0, 0)
    else:
        lax.fori_loop(0, d, residue, 0, unroll=unroll)


def _dilated_branch(rd, a_tab, b, s, d):
    ln = s // d
    tq = TQ_DIL
    rows, unroll = DIL_STEP[d]
    rows = min(rows, ln)
    nq = ln // rows
    edge = rows // BAND

    def cur(sec):
        return pl.BlockSpec((None, None, d, rows, SEC), lambda bi, qi, hg: (sec + hg, bi, 0, qi, 0))

    def prev(sec):
        return pl.BlockSpec((None, None, d, BAND, SEC),
                            lambda bi, qi, hg: (sec + hg, bi, 0, jnp.maximum(edge * qi - 1, 0), 0))

    def nxt(sec):
        return pl.BlockSpec((None, None, d, BAND, SEC),
                            lambda bi, qi, hg: (sec + hg, bi, 0, jnp.minimum(edge * (qi + 1), edge * nq - 1), 0))

    return pl.pallas_call(
        functools.partial(_dilated_kernel, nq=nq, unroll=unroll),
        grid=(b, nq, 2),
        in_specs=[cur(SEC_QA), prev(SEC_KA), cur(SEC_KA), nxt(SEC_KA), prev(SEC_VA), cur(SEC_VA), nxt(SEC_VA),
                  pl.BlockSpec((4, None, 4, tq, tq + 2 * BAND), lambda bi, qi, hg: (0, hg, 0, 0, 0))],
        out_specs=[pl.BlockSpec((None, 4, d * rows, HEAD_DIM), lambda bi, qi, hg: (bi, hg, qi, 0)),
                   pl.BlockSpec((None, None, d * rows, LANES), lambda bi, qi, hg: (hg, bi, qi, 0))],
        out_shape=[jax.ShapeDtypeStruct((b, N_HEADS_A, s, HEAD_DIM), F32),
                   jax.ShapeDtypeStruct((2, b, s, LANES), F32)],
        compiler_params=_cparams(3),
        name=f"dilated_d{d}",
    )(rd, rd, rd, rd, rd, rd, rd, a_tab)


def _t5_buckets(rel):
    nb = NUM_BUCKETS // 2
    max_exact = nb // 2
    n = np.abs(rel)
    large = max_exact + (np.log(np.maximum(n, 1) / max_exact) / np.log(MAX_DISTANCE / max_exact)
                         * (nb - max_exact)).astype(np.int32)
    large = np.minimum(large, nb - 1)
    return (rel > 0).astype(np.int32) * nb + np.where(n < max_exact, n, large).astype(np.int32)


def _band_tables(rel_bias, d):
    tq = TQ_DIL
    width = tq + 2 * BAND
    rel = np.arange(width + tq - 1) - (tq - 1) - BAND
    bucket = _t5_buckets(d * np.clip(rel, -BAND, BAND))
    diag = jnp.where(np.abs(rel) <= BAND, rel_bias.astype(F32)[bucket].T, NEG_INF)
    n = diag.shape[1]
    flat = jnp.tile(diag, (1, tq))[:, tq - 1:tq - 1 + tq * (n - 1)]
    table = flat.reshape(N_HEADS_A, tq, n - 1)[:, :, :width]
    c = np.arange(width)[None, None, :]
    prev_ok, next_ok = c >= BAND, c < BAND + tq
    tabs = [jnp.where(keep, table, NEG_INF) for keep in (prev_ok, c >= 0, next_ok, prev_ok & next_ok)]
    return jnp.stack(tabs).reshape(4, 2, 4, tq, width)


def _gqa_kernel(q_ref, k_ref, v_ref, o_ref, vx_ref, *, tk, unroll):
    tq = q_ref.shape[0]
    nk = k_ref.shape[0] // tk
    rep = SEC // HEAD_DIM

    @pl.when(pl.program_id(2) == 0)
    def _():
        vx_ref[:, :HEAD_DIM] = v_ref[...]
        vx_ref[:, HEAD_DIM:] = jnp.ones((vx_ref.shape[0], HEAD_DIM), BF16)

    q = jnp.concatenate([q_ref[:, hh * HEAD_DIM:(hh + 1) * HEAD_DIM] for hh in range(rep)], axis=0)
    nt = (((1,), (1,)), ((), ()))

    def body(c, carry):
        m, l, acc = carry
        start = pl.multiple_of(c * tk, tk)
        s = lax.dot_general(q, k_ref[pl.ds(start, tk), :], nt, preferred_element_type=F32)
        m_new = jnp.maximum(m, jnp.max(s, axis=-1, keepdims=True))
        alpha = jnp.exp2(m - m_new)
        p = jnp.exp2((s - m_new).astype(BF16))
        pv = jnp.dot(p, vx_ref[pl.ds(start, tk), :], preferred_element_type=F32)
        l = alpha * l + pv[:, HEAD_DIM:HEAD_DIM + 1]
        acc = alpha * acc + pv[:, :HEAD_DIM]
        return m_new, l, acc

    init = (jnp.full((rep * tq, 1), NEG_INF, F32), jnp.zeros((rep * tq, 1), F32),
            jnp.zeros((rep * tq, HEAD_DIM), F32))
    _, l, acc = lax.fori_loop(0, nk, body, init, unroll=unroll)
    o = acc / l
    for hh in range(rep):
        o_ref[:, hh * HEAD_DIM:(hh + 1) * HEAD_DIM] = o[hh * tq:(hh + 1) * tq]


def _gqa(qkv, b, s):
    tq = TQ_GQA
    tk = min(TK_GQA, s)
    view = qkv.reshape(N_SEC, b, s, SEC)
    o = pl.pallas_call(
        functools.partial(_gqa_kernel, tk=tk, unroll=min(UNROLL_GQA, s // tk)),
        grid=(b, N_KV_B, s // tq),
        in_specs=[pl.BlockSpec((None, None, tq, SEC), lambda bi, g, qi: (SEC_QB + g, bi, qi, 0)),
                  pl.BlockSpec((None, None, s, HEAD_DIM), lambda bi, g, qi: (SEC_KVB, bi, 0, g)),
                  pl.BlockSpec((None, None, s, HEAD_DIM), lambda bi, g, qi: (SEC_KVB, bi, 0, N_KV_B + g))],
        out_specs=pl.BlockSpec((None, tq, SEC), lambda bi, g, qi: (bi, qi, g)),
        out_shape=jax.ShapeDtypeStruct((b, s, WIDTH_B), F32),
        scratch_shapes=[pltpu.VMEM((s, 2 * HEAD_DIM), BF16)],
        compiler_params=_cparams(3),
        name="gqa",
    )(view, view, view)
    return o.reshape(b * s, WIDTH_B)


def _out_proj_kernel(o1_ref, o2_ref, o3_ref, l1_ref, l2_ref, l3_ref, ob_ref, x_ref, ga_ref, gb_ref, w_ref,
                     g2_ref, wr_ref, br_ref, run0_ref, y_ref, h_ref, meta_ref, cnt_ref, run_ref):
    @pl.when(pl.program_id(0) == 0)
    def _():
        run_ref[...] = run0_ref[...]

    lses = (l1_ref[...], l2_ref[...], l3_ref[...])
    m = jnp.maximum(jnp.maximum(lses[0], lses[1]), lses[2])
    es = [jnp.exp(v - m) for v in lses]
    tot = es[0] + es[1] + es[2]
    ws = [e / tot for e in es]
    heads = []
    for h in range(N_HEADS_A):
        hg, hl = divmod(h, 4)
        acc = None
        for w, o_ref in zip(ws, (o1_ref, o2_ref, o3_ref)):
            term = w[hg][:, hl:hl + 1] * o_ref[h]
            acc = term if acc is None else acc + term
        heads.append(acc)
    oa = jnp.concatenate(heads, axis=1)
    oa = oa * lax.rsqrt(jnp.mean(oa * oa, axis=-1, keepdims=True) + EPS) * ga_ref[...]
    ob = ob_ref[...]
    ob = ob * lax.rsqrt(jnp.mean(ob * ob, axis=-1, keepdims=True) + EPS) * gb_ref[...]
    y = (jnp.dot(oa.astype(BF16), w_ref[:WIDTH_A, :], preferred_element_type=F32)
         + jnp.dot(ob.astype(BF16), w_ref[WIDTH_A:, :], preferred_element_type=F32))
    x2 = x_ref[...] + y
    y_ref[...] = x2
    h, meta, run = _route_tile(x2, g2_ref[...], wr_ref[...], br_ref[...], run_ref[...])
    h_ref[...] = _pack_halves(h)
    meta_ref[...] = meta
    run_ref[...] = run
    cnt_ref[...] = run


def _out_proj(os_, lses, ob, x, on_a, on_b, w_out, g2, w_r, b_r, run0, s):
    t = x.shape[0]
    tm = TM_OUT
    spt = s // tm
    row = lambda width: pl.BlockSpec((tm, width), lambda i: (i, 0))
    o_spec = pl.BlockSpec((None, N_HEADS_A, tm, HEAD_DIM), lambda i: (i // spt, 0, i % spt, 0))
    lse_spec = pl.BlockSpec((2, None, tm, LANES), lambda i: (0, i // spt, i % spt, 0))
    const = lambda shape: pl.BlockSpec(shape, lambda i: (0, 0))
    return pl.pallas_call(
        _out_proj_kernel,
        grid=(t // tm,),
        in_specs=[o_spec, o_spec, o_spec, lse_spec, lse_spec, lse_spec, row(WIDTH_B),
                  row(D_MODEL), const((1, WIDTH_A)), const((1, WIDTH_B)), const((WIDTH_A + WIDTH_B, D_MODEL)),
                  const((1, D_MODEL)), const((D_MODEL, 2 * LANES)), const((1, 2 * LANES)), const((1, LANES))],
        out_specs=[row(D_MODEL), row(D_PACK), row(LANES), const((1, LANES))],
        out_shape=[jax.ShapeDtypeStruct((t, D_MODEL), F32), jax.ShapeDtypeStruct((t, D_PACK), jnp.uint32),
                   jax.ShapeDtypeStruct((t, LANES), F32), jax.ShapeDtypeStruct((1, LANES), F32)],
        scratch_shapes=[pltpu.VMEM((1, LANES), F32)],
        compiler_params=_cparams(1),
        name="out_proj",
    )(*os_, *lses, ob, x, on_a, on_b, w_out, g2, w_r, b_r, run0)


def _route_tile(x, g2, w_r, b_r, run):
    tm = x.shape[0]
    h = x * lax.rsqrt(jnp.mean(x * x, axis=-1, keepdims=True) + EPS) * g2
    logits = jnp.dot(h.astype(BF16), w_r, preferred_element_type=F32) + b_r
    gl, el = logits[:, :LANES], logits[:, LANES:]
    lane = lax.broadcasted_iota(jnp.int32, (tm, LANES), 1).astype(F32)
    ninf = jnp.float32(-jnp.inf)
    big = jnp.float32(LANES)

    def first_argmax(v):
        top = jnp.max(v, axis=-1, keepdims=True)
        return top, jnp.min(jnp.where(v == top, lane, big), axis=-1, keepdims=True)

    gmask = lane < N_GROUPS
    gtop, gsel = first_argmax(jnp.where(gmask, gl, ninf))
    g_w = 1.0 / jnp.sum(jnp.where(gmask, jnp.exp(gl - gtop), 0.0), axis=-1, keepdims=True)
    lo = gsel * EXPERTS_PER_GROUP
    elm = jnp.where((lane >= lo) & (lane < lo + EXPERTS_PER_GROUP), el, ninf)
    t1, i1 = first_argmax(elm)
    t2, i2 = first_argmax(jnp.where(lane == i1, ninf, elm))
    e2 = jnp.exp(t2 - t1)
    gate1 = g_w * (1.0 / (1.0 + e2))
    gate2 = g_w * (e2 / (1.0 + e2))

    hot1, hot2 = lane == i1, lane == i2
    hot = (hot1 | hot2).astype(F32)
    r_i = lax.broadcasted_iota(jnp.int32, (tm, tm), 0)
    c_i = lax.broadcasted_iota(jnp.int32, (tm, tm), 1)
    before = (c_i < r_i).astype(BF16)
    rank = jnp.dot(before, hot.astype(BF16), preferred_element_type=F32) + run
    rank1 = jnp.sum(jnp.where(hot1, rank, 0.0), axis=-1, keepdims=True)
    rank2 = jnp.sum(jnp.where(hot2, rank, 0.0), axis=-1, keepdims=True)
    run = run + jnp.sum(hot, axis=0, keepdims=True)

    meta = jnp.zeros((tm, LANES), F32)
    for k, v in enumerate((i1, i2, rank1, rank2, gate1, gate2)):
        meta = jnp.where(lane == k, v, meta)
    return h, meta, run


def _dispatch_kernel(plan_ref, dest_ref, ha_ref, hb_ref, xs_ref, zero_ref, zsem, sem, *, steps_a):
    i = pl.program_id(0)
    tt = dest_ref.shape[2] // TOP_K
    rows = zero_ref.shape[0]
    p_rows = xs_ref.shape[0]

    def pad_block(e):
        return plan_ref[N_EXPERTS + e] - rows, plan_ref[e] > 0

    def tail_block(c):
        row0 = plan_ref[2 * N_EXPERTS - 1] + c * rows
        return row0, row0 < p_rows

    def for_zero_blocks(block, action):
        def body(c, carry):
            row0, live = block(c)

            @pl.when(live)
            def _():
                dst = xs_ref.at[pl.ds(pl.multiple_of(row0, rows), rows)]
                action(pltpu.make_async_copy(zero_ref, dst, zsem))
            return carry

        lax.fori_loop(0, N_EXPERTS, body, 0)

    @pl.when(i == 0)
    def _():
        zero_ref[...] = jnp.zeros_like(zero_ref)
        for block in (pad_block, tail_block):
            for_zero_blocks(block, lambda cp: cp.start())
        for block in (pad_block, tail_block):
            for_zero_blocks(block, lambda cp: cp.wait())

    def scatter_rows(h_ref):
        def row_copy(tok, k):
            dst = dest_ref[0, 0, TOP_K * tok + k]
            return pltpu.make_async_copy(h_ref.at[pl.ds(tok, 1)], xs_ref.at[pl.ds(dst, 1)], sem)

        def start_rows(tok, c):
            for k in range(TOP_K):
                row_copy(tok, k).start()
            return c

        def wait_rows(tok, c):
            for k in range(TOP_K):
                row_copy(tok, k).wait()
            return c

        lax.fori_loop(0, tt, start_rows, 0, unroll=ROW_UNROLL)
        lax.fori_loop(0, tt, wait_rows, 0, unroll=ROW_UNROLL)

    @pl.when(i < steps_a)
    def _():
        scatter_rows(ha_ref)

    @pl.when(i >= steps_a)
    def _():
        scatter_rows(hb_ref)


def _dispatch(plan, dest, h_a, h_b, p_rows):
    tt = TT_ROWS
    steps_a, steps_b = h_a.shape[0] // tt, h_b.shape[0] // tt
    return pl.pallas_call(
        functools.partial(_dispatch_kernel, steps_a=steps_a),
        grid_spec=pltpu.PrefetchScalarGridSpec(
            num_scalar_prefetch=1,
            grid=(steps_a + steps_b,),
            in_specs=[pl.BlockSpec((1, 1, TOP_K * tt), lambda i, plan: (i, 0, 0), memory_space=pltpu.SMEM),
                      pl.BlockSpec((tt, D_PACK), lambda i, plan: (jnp.minimum(i, steps_a - 1), 0)),
                      pl.BlockSpec((tt, D_PACK), lambda i, plan: (jnp.maximum(i - steps_a, 0), 0))],
            out_specs=pl.BlockSpec(memory_space=pl.ANY),
            scratch_shapes=[pltpu.VMEM((MOE_ROWS, D_PACK), jnp.uint32), pltpu.SemaphoreType.DMA(()),
                            pltpu.SemaphoreType.DMA(())]),
        out_shape=jax.ShapeDtypeStruct((p_rows, D_PACK), jnp.uint32),
        compiler_params=_cparams(1),
        name="dispatch",
    )(plan, dest, h_a, h_b)


def _experts_kernel(sched_ref, nused_ref, xs_ref, wg_hbm, wu_hbm, wd_hbm, ys_ref,
                    wg_st, wu_st, wd_st, wg_bf, wu_bf, wd_bf, sem):
    i = pl.program_id(0)
    live = i < nused_ref[0]
    nblk = pl.num_programs(0)
    expert, first, nxt = (sched_ref[r * nblk + i] for r in range(3))
    pairs = ((wg_hbm, wg_st, wg_bf), (wu_hbm, wu_st, wu_bf), (wd_hbm, wd_st, wd_bf))

    def stage_copies(e):
        return [pltpu.make_async_copy(src.at[e], st, sem) for src, st, _ in pairs]

    @pl.when(i == 0)
    def _():
        for cp in stage_copies(expert):
            cp.start()

    @pl.when(live & (first == 1))
    def _():
        for cp in stage_copies(expert):
            cp.wait()
        for _, st, bf in pairs:
            chunk = CAST_ROWS * D_EXPERT // st.shape[1]

            def cast(c, carry, st=st, bf=bf, chunk=chunk):
                rows = pl.ds(pl.multiple_of(c * chunk, chunk), chunk)
                bf[rows, :] = st[rows, :].astype(BF16)
                return carry

            lax.fori_loop(0, st.shape[0] // chunk, cast, 0)

        @pl.when(nxt >= 0)
        def _():
            for cp in stage_copies(nxt):
                cp.start()

    @pl.when(live)
    def _():
        lo, hi = (v.astype(BF16) for v in _unpack_halves(xs_ref[...]))
        up = lambda w: (jnp.dot(lo, w[:D_PACK, :], preferred_element_type=F32)
                        + jnp.dot(hi, w[D_PACK:, :], preferred_element_type=F32))
        g, u = up(wg_bf), up(wu_bf)
        hid = g * (1.0 / (1.0 + jnp.exp(-g))) * u
        ys_ref[...] = _pack_halves(jnp.dot(hid.astype(BF16), wd_bf[...], preferred_element_type=F32))

    @pl.when(jnp.logical_not(live))
    def _():
        ys_ref[...] = jnp.zeros_like(ys_ref)


def _experts(sched, nused, xs, w_gate, w_up, w_down):
    p_rows = xs.shape[0]
    rows = MOE_ROWS
    live = lambda i, sched, nused: jnp.minimum(i, jnp.maximum(nused[0] - 1, 0))
    hbm = pl.BlockSpec(memory_space=pl.ANY)
    return pl.pallas_call(
        _experts_kernel,
        grid_spec=pltpu.PrefetchScalarGridSpec(
            num_scalar_prefetch=2,
            grid=(p_rows // rows,),
            in_specs=[pl.BlockSpec((rows, D_PACK), lambda i, sched, nused: (live(i, sched, nused), 0)), hbm, hbm, hbm],
            out_specs=pl.BlockSpec((rows, D_PACK), lambda i, sched, nused: (i, 0)),
            scratch_shapes=[pltpu.VMEM((D_MODEL, D_EXPERT), F32), pltpu.VMEM((D_MODEL, D_EXPERT), F32),
                            pltpu.VMEM((D_EXPERT, D_MODEL), F32),
                            pltpu.VMEM((D_MODEL, D_EXPERT), BF16), pltpu.VMEM((D_MODEL, D_EXPERT), BF16),
                            pltpu.VMEM((D_EXPERT, D_MODEL), BF16), pltpu.SemaphoreType.DMA(())]),
        out_shape=jax.ShapeDtypeStruct((p_rows, D_PACK), jnp.uint32),
        compiler_params=_cparams(1),
        name="experts",
    )(sched, nused, xs, w_gate, w_up, w_down)


def _combine_kernel(dest_ref, next_ref, x_ref, meta_ref, ys_ref, o_ref, buf_ref, sems):
    i = pl.program_id(0)
    tt = x_ref.shape[0]
    slot = i % 2

    def for_rows(ids_ref, sl, action):
        def body(tok, c):
            for k in range(TOP_K):
                src = ids_ref[0, 0, TOP_K * tok + k]
                action(pltpu.make_async_copy(ys_ref.at[pl.ds(src, 1)], buf_ref.at[sl, k, pl.ds(tok, 1)], sems.at[sl]))
            return c

        lax.fori_loop(0, tt, body, 0, unroll=ROW_UNROLL)

    @pl.when(i == 0)
    def _():
        for_rows(dest_ref, slot, lambda cp: cp.start())

    @pl.when(i + 1 < pl.num_programs(0))
    def _():
        for_rows(next_ref, 1 - slot, lambda cp: cp.start())

    for_rows(dest_ref, slot, lambda cp: cp.wait())
    meta = meta_ref[...]
    g1, g2 = meta[:, 4:5], meta[:, 5:6]
    (lo1, hi1), (lo2, hi2) = _unpack_halves(buf_ref[slot, 0]), _unpack_halves(buf_ref[slot, 1])
    o_ref[:, :D_PACK] = x_ref[:, :D_PACK] + (g1 * lo1 + g2 * lo2)
    o_ref[:, D_PACK:] = x_ref[:, D_PACK:] + (g1 * hi1 + g2 * hi2)


def _combine(dest, x2, meta, ys):
    t = x2.shape[0]
    tt = TT_ROWS
    n = t // tt
    return pl.pallas_call(
        _combine_kernel,
        grid=(n,),
        in_specs=[pl.BlockSpec((1, 1, TOP_K * tt), lambda i: (i, 0, 0), memory_space=pltpu.SMEM),
                  pl.BlockSpec((1, 1, TOP_K * tt), lambda i: (jnp.minimum(i + 1, n - 1), 0, 0),
                               memory_space=pltpu.SMEM),
                  pl.BlockSpec((tt, D_MODEL), lambda i: (i, 0)),
                  pl.BlockSpec((tt, LANES), lambda i: (i, 0)),
                  pl.BlockSpec(memory_space=pl.ANY)],
        out_specs=pl.BlockSpec((tt, D_MODEL), lambda i: (i, 0)),
        out_shape=jax.ShapeDtypeStruct((t, D_MODEL), F32),
        scratch_shapes=[pltpu.VMEM((2, TOP_K, tt, D_PACK), jnp.uint32), pltpu.SemaphoreType.DMA((2,))],
        compiler_params=_cparams(1),
        name="combine",
    )(dest, dest, x2, meta, ys)


def _rope_tables(s):
    rows = s // GRID_W
    r, c = jnp.meshgrid(jnp.arange(rows, dtype=F32), jnp.arange(GRID_W, dtype=F32), indexing='ij')
    n_freq = HEAD_DIM // 4
    inv = ROPE_THETA ** (-jnp.arange(n_freq, dtype=F32) / n_freq)
    ang_r = r.reshape(-1)[:, None] * inv
    ang_c = c.reshape(-1)[:, None] * inv
    ang = jnp.concatenate([ang_r, ang_r, ang_c, ang_c], axis=-1)
    sign = np.tile(np.repeat(np.array([-1.0, 1.0], np.float32), n_freq), 2)
    return jnp.cos(ang), jnp.sin(ang) * sign


def _moe_plan(metas, cnt):
    rows = MOE_ROWS
    t = sum(m.shape[0] for m in metas)
    p_rows = TOP_K * t + N_EXPERTS * rows
    nblk = p_rows // rows
    counts = cnt[0, :N_EXPERTS].astype(jnp.int32)
    pcounts = (counts + rows - 1) // rows * rows
    pends = jnp.cumsum(pcounts)
    pstarts = pends - pcounts
    ids = jnp.arange(N_EXPERTS, dtype=jnp.int32)

    def slots(meta):
        expert = meta[:, 0:TOP_K].astype(jnp.int32)
        rank = meta[:, TOP_K:2 * TOP_K].astype(jnp.int32)
        start = jnp.sum(jnp.where(expert[..., None] == ids, pstarts, 0), axis=-1)
        return (start + rank).reshape(meta.shape[0] // TT_ROWS, 1, TOP_K * TT_ROWS)

    dest = [slots(m) for m in metas]
    nused = pends[-1] // rows
    row0 = jnp.arange(nblk, dtype=jnp.int32) * rows
    blk = jnp.sum(pends[None, :] <= jnp.minimum(row0, pends[-1] - rows)[:, None], axis=1).astype(jnp.int32)
    plan = jnp.concatenate([pcounts, pends]).astype(jnp.int32)
    blocks = jnp.arange(nblk, dtype=jnp.int32)
    first = ((blocks == 0) | (blk != jnp.roll(blk, 1))) & (blocks < nused)
    later = (pcounts[None, :] > 0) & (ids[None, :] > ids[:, None])
    next_used = jnp.min(jnp.where(later, ids[None, :], N_EXPERTS), axis=1)
    next_used = jnp.where(next_used == N_EXPERTS, -1, next_used)
    nxt = jnp.sum(jnp.where(blk[:, None] == ids[None, :], next_used[None, :], 0), axis=1)
    sched = jnp.concatenate([blk, first.astype(jnp.int32), nxt]).astype(jnp.int32)
    return plan, dest, sched, nused.reshape(1).astype(jnp.int32), p_rows


def _attend_and_route(x, p, run0):
    b, s, _ = x.shape
    t = b * s
    assert s % (TQ_DIL * max(d for _, d in DILATED)) == 0 and s % TM_IN == 0 and t % TT_ROWS == 0
    x = x.reshape(t, D_MODEL)
    cos, sin = _rope_tables(s)
    qkv, r4, r16 = _in_proj(x, p['norm1_g'], p['w_in'], p['gain'], cos, sin, b, s)
    residue_major = {1: qkv.reshape(N_SEC, b, 1, s, SEC), 4: r4, 16: r16}
    branches = [_dilated_branch(residue_major[d], p['band'][i], b, s, d) for i, (_, d) in enumerate(DILATED)]
    ob = _gqa(qkv, b, s)
    return _out_proj([o for o, _ in branches], [l for _, l in branches], ob, x, p['on_a'], p['on_b'],
                     p['w_out'], p['norm2_g'], p['w_r'], p['b_r'], run0, s)


def _layer(x_a, x_b, p):
    x2_a, h_a, meta_a, cnt_a = _attend_and_route(x_a, p, jnp.zeros((1, LANES), F32))
    x2_b, h_b, meta_b, cnt = _attend_and_route(x_b, p, cnt_a)
    plan, (dest_a, dest_b), sched, nused, p_rows = _moe_plan([meta_a, meta_b], cnt)
    xs = _dispatch(plan, jnp.concatenate([dest_a, dest_b]), h_a, h_b, p_rows)
    ys = _experts(sched, nused, xs, p['w_gate'], p['w_up'], p['w_down'])
    return (_combine(dest_a, x2_a, meta_a, ys).reshape(x_a.shape),
            _combine(dest_b, x2_b, meta_b, ys).reshape(x_b.shape))


def kernel(x_prompt, x_sample, norm1_g, w_in, qn_a, kn_a, qn_b, kn_b, rel_bias, on_a, on_b, w_out, norm2_g, rg_w,
           rg_b, re_w, re_b, w_gate, w_up, w_down):
    assert norm1_g.shape[0] == 1, "one layer"
    scale = HEAD_DIM ** -0.5
    ones_a = jnp.ones((WIDTH_A,), F32)
    gain = jnp.concatenate([jnp.tile(qn_a[0], N_HEADS_A) * scale, jnp.tile(kn_a[0], N_HEADS_A), ones_a,
                            jnp.tile(qn_b[0], N_HEADS_B) * (scale * LOG2E), jnp.tile(kn_b[0], N_KV_B),
                            jnp.ones((KV_WIDTH_B,), F32)]).reshape(1, IN_WIDTH)
    pad_g = jnp.zeros((D_MODEL, LANES - N_GROUPS), F32)
    pad_e = jnp.zeros((D_MODEL, LANES - N_EXPERTS), F32)
    params = dict(
        norm1_g=norm1_g, w_in=w_in[0].astype(BF16), gain=gain,
        band=[_band_tables(rel_bias, d) for _, d in DILATED],
        on_a=on_a, on_b=on_b, w_out=w_out[0].astype(BF16), norm2_g=norm2_g,
        w_r=jnp.concatenate([rg_w[0], pad_g, re_w[0], pad_e], axis=1).astype(BF16),
        b_r=jnp.concatenate([rg_b[0], jnp.zeros((LANES - N_GROUPS,), F32), re_b[0],
                             jnp.zeros((LANES - N_EXPERTS,), F32)]).reshape(1, 2 * LANES),
        w_gate=w_gate[0], w_up=w_up[0], w_down=w_down[0])
    return _layer(x_prompt, x_sample, params)
```

```python
import functools

import numpy as np
import jax
import jax.numpy as jnp
from jax import lax
from jax.experimental import pallas as pl
from jax.experimental.pallas import tpu as pltpu

D_MODEL = 2048
HEAD_DIM = 128
N_HEADS_A = 8
N_HEADS_B = 8
N_KV_B = 2
WIDTH_A = N_HEADS_A * HEAD_DIM
WIDTH_B = N_HEADS_B * HEAD_DIM
KV_WIDTH_B = N_KV_B * HEAD_DIM
IN_WIDTH = 3 * WIDTH_A + WIDTH_B + 2 * KV_WIDTH_B
DILATED = ((128, 1), (512, 4), (2048, 16))
GRID_W = 64
ROPE_THETA = 10000.0
NUM_BUCKETS = 32
MAX_DISTANCE = 1024
N_GROUPS = 4
EXPERTS_PER_GROUP = 8
N_EXPERTS = N_GROUPS * EXPERTS_PER_GROUP
TOP_K = 2
D_EXPERT = D_MODEL // 2
D_PACK = D_MODEL // 2
EPS = 1e-6
NEG_INF = -1e30
LOG2E = float(np.log2(np.e))

LANES = 128
SEC = 4 * HEAD_DIM
N_SEC = IN_WIDTH // SEC
SEC_GROUP = 3
SEC_QA, SEC_KA, SEC_VA, SEC_QB, SEC_KVB = 0, 2, 4, 6, 8
BAND = 64
VMEM_LIMIT = 56 * 1024 * 1024

TM_IN = 512
TQ_DIL = 128
DIL_STEP = {1: (1024, 1), 4: (256, 4), 16: (128, 8)}
TQ_GQA = 256
TK_GQA = 512
UNROLL_GQA = 16
TM_OUT = 256
TT_ROWS = 256
ROW_UNROLL = 8
MOE_ROWS = 256
CAST_ROWS = 256

F32 = jnp.float32
BF16 = jnp.bfloat16


def _cparams(n_axes):
    return pltpu.CompilerParams(dimension_semantics=("arbitrary",) * n_axes, vmem_limit_bytes=VMEM_LIMIT)


def _pack_halves(a):
    n = a.shape[1] // 2
    lo = pltpu.bitcast(a[:, :n].astype(BF16).astype(F32), jnp.uint32)
    hi = pltpu.bitcast(a[:, n:].astype(BF16).astype(F32), jnp.uint32)
    return (lo >> 16) | (hi & jnp.uint32(0xFFFF0000))


def _unpack_halves(u):
    return pltpu.bitcast(u << 16, F32), pltpu.bitcast(u & jnp.uint32(0xFFFF0000), F32)


def _in_proj_kernel(x_ref, g1_ref, w_ref, gain_ref, cos_ref, sin_ref, o_ref, r4_ref, r16_ref, h_ref, y_ref, y4_ref):
    jp = pl.program_id(1)
    tm = x_ref.shape[0]
    heads = SEC // HEAD_DIM

    @pl.when(jp == 0)
    def _():
        x = x_ref[...]
        ms = jnp.mean(x * x, axis=-1, keepdims=True)
        h_ref[...] = (x * lax.rsqrt(ms + EPS) * g1_ref[...]).astype(BF16)

    lane = lax.broadcasted_iota(jnp.int32, (tm, HEAD_DIM), 1)
    first_quarter = (lane & 32) == 0

    def section(half, norm, rope, residue_major=False):
        p = jnp.dot(h_ref[...], w_ref[:, half * SEC:(half + 1) * SEC], preferred_element_type=F32)
        for hh in range(heads):
            sl = slice(hh * HEAD_DIM, (hh + 1) * HEAD_DIM)
            y = p[:, sl]
            if norm[hh]:
                ms = jnp.mean(y * y, axis=-1, keepdims=True)
                y = y * lax.rsqrt(ms + EPS)
            y = y * gain_ref[:, half * SEC + hh * HEAD_DIM:half * SEC + (hh + 1) * HEAD_DIM]
            if rope[hh]:
                partner = jnp.where(first_quarter, pltpu.roll(y, HEAD_DIM - 32, 1), pltpu.roll(y, 32, 1))
                y = y * cos_ref[...] + partner * sin_ref[...]
            o_ref[half, :, sl] = y.astype(BF16)
            if residue_major:
                slab = half * heads + hh
                y_ref[slab] = y
                for r in range(4):
                    y4 = y_ref[slab, pl.ds(r, tm // 4, stride=4), :]
                    r4_ref[half, r, :, sl] = y4.astype(BF16)
                    y4_ref[slab, pl.ds(r * (tm // 4), tm // 4), :] = y4
                for r in range(4):
                    for q in range(4):
                        y16 = y4_ref[slab, pl.ds(r * (tm // 4) + q, tm // 16, stride=4), :]
                        r16_ref[half, 4 * q + r, :, sl] = y16.astype(BF16)

    yes, no = (True,) * heads, (False,) * heads
    kv_b = (True,) * N_KV_B + (False,) * (heads - N_KV_B)

    def flags(sec):
        if sec < SEC_VA:
            return yes, no, True
        if sec < SEC_QB:
            return no, no, True
        if sec < SEC_KVB:
            return yes, yes, False
        return kv_b, kv_b, False

    for group in range(N_SEC // SEC_GROUP):
        @pl.when(jp == group)
        def _(group=group):
            for part in range(SEC_GROUP):
                section(part, *flags(group * SEC_GROUP + part))


def _in_proj(x, g1, w_in, gain, cos, sin, b, s):
    t = x.shape[0]
    tm = TM_IN
    spt = s // tm
    grp = SEC_GROUP
    n_dil = SEC_QB // grp

    def residue_major(d):
        spec = pl.BlockSpec((grp, None, d, tm // d, SEC),
                            lambda i, jp: (jnp.minimum(jp, n_dil - 1), i // spt, 0, i % spt, 0))
        return spec, jax.ShapeDtypeStruct((grp * n_dil, b, d, s // d, SEC), BF16)

    (r4_spec, r4_shape), (r16_spec, r16_shape) = residue_major(4), residue_major(16)
    return pl.pallas_call(
        _in_proj_kernel,
        grid=(t // tm, N_SEC // grp),
        in_specs=[
            pl.BlockSpec((tm, D_MODEL), lambda i, jp: (i, 0)),
            pl.BlockSpec((1, D_MODEL), lambda i, jp: (0, 0)),
            pl.BlockSpec((D_MODEL, grp * SEC), lambda i, jp: (0, jp)),
            pl.BlockSpec((1, grp * SEC), lambda i, jp: (0, jp)),
            pl.BlockSpec((tm, HEAD_DIM), lambda i, jp: (i % spt, 0)),
            pl.BlockSpec((tm, HEAD_DIM), lambda i, jp: (i % spt, 0)),
        ],
        out_specs=[pl.BlockSpec((grp, tm, SEC), lambda i, jp: (jp, i, 0)), r4_spec, r16_spec],
        out_shape=[jax.ShapeDtypeStruct((N_SEC, t, SEC), BF16), r4_shape, r16_shape],
        scratch_shapes=[pltpu.VMEM((tm, D_MODEL), BF16), pltpu.VMEM((grp * SEC // HEAD_DIM, tm, HEAD_DIM), F32),
                        pltpu.VMEM((grp * SEC // HEAD_DIM, tm, HEAD_DIM), F32)],
        compiler_params=_cparams(2),
        name="in_proj",
    )(x, g1, w_in, gain, cos, sin)


def _dilated_kernel(q_ref, kp_ref, kc_ref, kn_ref, vp_ref, vc_ref, vn_ref, a_ref, o_ref, lse_ref, *, nq, unroll):
    d, rows = q_ref.shape[0], q_ref.shape[1]
    tq = TQ_DIL
    nb = rows // tq
    lane = lax.broadcasted_iota(jnp.int32, (tq, LANES), 1)
    nt = (((1,), (1,)), ((), ()))
    first, last = pl.program_id(1) == 0, pl.program_id(1) == nq - 1

    def variant(j):
        lo, hi = (first if j == 0 else False), (last if j == nb - 1 else False)
        if lo is False and hi is False:
            return 1
        if hi is False:
            return jnp.where(lo, 0, 1)
        if lo is False:
            return jnp.where(hi, 2, 1)
        return jnp.where(lo & hi, 3, jnp.where(lo, 0, jnp.where(hi, 2, 1)))

    def residue(r, carry):
        heads = range(SEC // HEAD_DIM)
        sls = [slice(hh * HEAD_DIM, (hh + 1) * HEAD_DIM) for hh in heads]
        ks = [jnp.concatenate([kp_ref[r, :, sl], kc_ref[r, :, sl], kn_ref[r, :, sl]], axis=0) for sl in sls]
        vs = [jnp.concatenate([vp_ref[r, :, sl], vc_ref[r, :, sl], vn_ref[r, :, sl]], axis=0) for sl in sls]
        for j in range(nb):
            out_rows = pl.ds(j * tq * d + r, tq, stride=d) if d > 1 else pl.ds(j * tq, tq)
            keys = slice(j * tq, (j + 1) * tq + 2 * BAND)
            var = variant(j)
            lse_all = jnp.zeros((tq, LANES), F32)
            for hh in heads:
                q = q_ref[r, j * tq:(j + 1) * tq, sls[hh]]
                s = lax.dot_general(q, ks[hh][keys], nt, preferred_element_type=F32) + a_ref[var, hh]
                m = jnp.max(s, axis=-1, keepdims=True)
                p = jnp.exp(s - m)
                l = jnp.sum(p, axis=-1, keepdims=True)
                o = jnp.dot(p.astype(BF16), vs[hh][keys], preferred_element_type=F32)
                o_ref[hh, out_rows, :] = o / l
                lse_all = jnp.where(lane == hh, m + jnp.log(l), lse_all)
            lse_ref[out_rows, :] = lse_all
        return carry

    if d == 1:
        residue(0, 0)
    else:
        lax.fori_loop(0, d, residue, 0, unroll=unroll)


def _dilated_branch(rd, a_tab, b, s, d):
    ln = s // d
    tq = TQ_DIL
    rows, unroll = DIL_STEP[d]
    rows = min(rows, ln)
    nq = ln // rows
    edge = rows // BAND

    def cur(sec):
        return pl.BlockSpec((None, None, d, rows, SEC), lambda bi, qi, hg: (sec + hg, bi, 0, qi, 0))

    def prev(sec):
        return pl.BlockSpec((None, None, d, BAND, SEC),
                            lambda bi, qi, hg: (sec + hg, bi, 0, jnp.maximum(edge * qi - 1, 0), 0))

    def nxt(sec):
        return pl.BlockSpec((None, None, d, BAND, SEC),
                            lambda bi, qi, hg: (sec + hg, bi, 0, jnp.minimum(edge * (qi + 1), edge * nq - 1), 0))

    return pl.pallas_call(
        functools.partial(_dilated_kernel, nq=nq, unroll=unroll),
        grid=(b, nq, 2),
        in_specs=[cur(SEC_QA), prev(SEC_KA), cur(SEC_KA), nxt(SEC_KA), prev(SEC_VA), cur(SEC_VA), nxt(SEC_VA),
                  pl.BlockSpec((4, None, 4, tq, tq + 2 * BAND), lambda bi, qi, hg: (0, hg, 0, 0, 0))],
        out_specs=[pl.BlockSpec((None, 4, d * rows, HEAD_DIM), lambda bi, qi, hg: (bi, hg, qi, 0)),
                   pl.BlockSpec((None, None, d * rows, LANES), lambda bi, qi, hg: (hg, bi, qi, 0))],
        out_shape=[jax.ShapeDtypeStruct((b, N_HEADS_A, s, HEAD_DIM), F32),
                   jax.ShapeDtypeStruct((2, b, s, LANES), F32)],
        compiler_params=_cparams(3),
        name=f"dilated_d{d}",
    )(rd, rd, rd, rd, rd, rd, rd, a_tab)


def _t5_buckets(rel):
    nb = NUM_BUCKETS // 2
    max_exact = nb // 2
    n = np.abs(rel)
    large = max_exact + (np.log(np.maximum(n, 1) / max_exact) / np.log(MAX_DISTANCE / max_exact)
                         * (nb - max_exact)).astype(np.int32)
    large = np.minimum(large, nb - 1)
    return (rel > 0).astype(np.int32) * nb + np.where(n < max_exact, n, large).astype(np.int32)


def _band_tables(rel_bias, d):
    tq = TQ_DIL
    width = tq + 2 * BAND
    rel = np.arange(width + tq - 1) - (tq - 1) - BAND
    bucket = _t5_buckets(d * np.clip(rel, -BAND, BAND))
    diag = jnp.where(np.abs(rel) <= BAND, rel_bias.astype(F32)[bucket].T, NEG_INF)
    n = diag.shape[1]
    flat = jnp.tile(diag, (1, tq))[:, tq - 1:tq - 1 + tq * (n - 1)]
    table = flat.reshape(N_HEADS_A, tq, n - 1)[:, :, :width]
    c = np.arange(width)[None, None, :]
    prev_ok, next_ok = c >= BAND, c < BAND + tq
    tabs = [jnp.where(keep, table, NEG_INF) for keep in (prev_ok, c >= 0, next_ok, prev_ok & next_ok)]
    return jnp.stack(tabs).reshape(4, 2, 4, tq, width)


def _gqa_kernel(q_ref, k_ref, v_ref, o_ref, vx_ref, *, tk, unroll):
    tq = q_ref.shape[0]
    nk = k_ref.shape[0] // tk
    rep = SEC // HEAD_DIM

    @pl.when(pl.program_id(2) == 0)
    def _():
        vx_ref[:, :HEAD_DIM] = v_ref[...]
        vx_ref[:, HEAD_DIM:] = jnp.ones((vx_ref.shape[0], HEAD_DIM), BF16)

    q = jnp.concatenate([q_ref[:, hh * HEAD_DIM:(hh + 1) * HEAD_DIM] for hh in range(rep)], axis=0)
    nt = (((1,), (1,)), ((), ()))

    def body(c, carry):
        m, l, acc = carry
        start = pl.multiple_of(c * tk, tk)
        s = lax.dot_general(q, k_ref[pl.ds(start, tk), :], nt, preferred_element_type=F32)
        m_new = jnp.maximum(m, jnp.max(s, axis=-1, keepdims=True))
        alpha = jnp.exp2(m - m_new)
        p = jnp.exp2((s - m_new).astype(BF16))
        pv = jnp.dot(p, vx_ref[pl.ds(start, tk), :], preferred_element_type=F32)
        l = alpha * l + pv[:, HEAD_DIM:HEAD_DIM + 1]
        acc = alpha * acc + pv[:, :HEAD_DIM]
        return m_new, l, acc

    init = (jnp.full((rep * tq, 1), NEG_INF, F32), jnp.zeros((rep * tq, 1), F32),
            jnp.zeros((rep * tq, HEAD_DIM), F32))
    _, l, acc = lax.fori_loop(0, nk, body, init, unroll=unroll)
    o = acc / l
    for hh in range(rep):
        o_ref[:, hh * HEAD_DIM:(hh + 1) * HEAD_DIM] = o[hh * tq:(hh + 1) * tq]


def _gqa(qkv, b, s):
    tq = TQ_GQA
    tk = min(TK_GQA, s)
    view = qkv.reshape(N_SEC, b, s, SEC)
    o = pl.pallas_call(
        functools.partial(_gqa_kernel, tk=tk, unroll=min(UNROLL_GQA, s // tk)),
        grid=(b, N_KV_B, s // tq),
        in_specs=[pl.BlockSpec((None, None, tq, SEC), lambda bi, g, qi: (SEC_QB + g, bi, qi, 0)),
                  pl.BlockSpec((None, None, s, HEAD_DIM), lambda bi, g, qi: (SEC_KVB, bi, 0, g)),
                  pl.BlockSpec((None, None, s, HEAD_DIM), lambda bi, g, qi: (SEC_KVB, bi, 0, N_KV_B + g))],
        out_specs=pl.BlockSpec((None, tq, SEC), lambda bi, g, qi: (bi, qi, g)),
        out_shape=jax.ShapeDtypeStruct((b, s, WIDTH_B), F32),
        scratch_shapes=[pltpu.VMEM((s, 2 * HEAD_DIM), BF16)],
        compiler_params=_cparams(3),
        name="gqa",
    )(view, view, view)
    return o.reshape(b * s, WIDTH_B)


def _out_proj_kernel(o1_ref, o2_ref, o3_ref, l1_ref, l2_ref, l3_ref, ob_ref, x_ref, ga_ref, gb_ref, w_ref,
                     g2_ref, wr_ref, br_ref, run0_ref, y_ref, h_ref, meta_ref, cnt_ref, run_ref):
    @pl.when(pl.program_id(0) == 0)
    def _():
        run_ref[...] = run0_ref[...]

    lses = (l1_ref[...], l2_ref[...], l3_ref[...])
    m = jnp.maximum(jnp.maximum(lses[0], lses[1]), lses[2])
    es = [jnp.exp(v - m) for v in lses]
    tot = es[0] + es[1] + es[2]
    ws = [e / tot for e in es]
    heads = []
    for h in range(N_HEADS_A):
        hg, hl = divmod(h, 4)
        acc = None
        for w, o_ref in zip(ws, (o1_ref, o2_ref, o3_ref)):
            term = w[hg][:, hl:hl + 1] * o_ref[h]
            acc = term if acc is None else acc + term
        heads.append(acc)
    oa = jnp.concatenate(heads, axis=1)
    oa = oa * lax.rsqrt(jnp.mean(oa * oa, axis=-1, keepdims=True) + EPS) * ga_ref[...]
    ob = ob_ref[...]
    ob = ob * lax.rsqrt(jnp.mean(ob * ob, axis=-1, keepdims=True) + EPS) * gb_ref[...]
    y = (jnp.dot(oa.astype(BF16), w_ref[:WIDTH_A, :], preferred_element_type=F32)
         + jnp.dot(ob.astype(BF16), w_ref[WIDTH_A:, :], preferred_element_type=F32))
    x2 = x_ref[...] + y
    y_ref[...] = x2
    h, meta, run = _route_tile(x2, g2_ref[...], wr_ref[...], br_ref[...], run_ref[...])
    h_ref[...] = _pack_halves(h)
    meta_ref[...] = meta
    run_ref[...] = run
    cnt_ref[...] = run


def _out_proj(os_, lses, ob, x, on_a, on_b, w_out, g2, w_r, b_r, run0, s):
    t = x.shape[0]
    tm = TM_OUT
    spt = s // tm
    row = lambda width: pl.BlockSpec((tm, width), lambda i: (i, 0))
    o_spec = pl.BlockSpec((None, N_HEADS_A, tm, HEAD_DIM), lambda i: (i // spt, 0, i % spt, 0))
    lse_spec = pl.BlockSpec((2, None, tm, LANES), lambda i: (0, i // spt, i % spt, 0))
    const = lambda shape: pl.BlockSpec(shape, lambda i: (0, 0))
    return pl.pallas_call(
        _out_proj_kernel,
        grid=(t // tm,),
        in_specs=[o_spec, o_spec, o_spec, lse_spec, lse_spec, lse_spec, row(WIDTH_B),
                  row(D_MODEL), const((1, WIDTH_A)), const((1, WIDTH_B)), const((WIDTH_A + WIDTH_B, D_MODEL)),
                  const((1, D_MODEL)), const((D_MODEL, 2 * LANES)), const((1, 2 * LANES)), const((1, LANES))],
        out_specs=[row(D_MODEL), row(D_PACK), row(LANES), const((1, LANES))],
        out_shape=[jax.ShapeDtypeStruct((t, D_MODEL), F32), jax.ShapeDtypeStruct((t, D_PACK), jnp.uint32),
                   jax.ShapeDtypeStruct((t, LANES), F32), jax.ShapeDtypeStruct((1, LANES), F32)],
        scratch_shapes=[pltpu.VMEM((1, LANES), F32)],
        compiler_params=_cparams(1),
        name="out_proj",
    )(*os_, *lses, ob, x, on_a, on_b, w_out, g2, w_r, b_r, run0)


def _route_tile(x, g2, w_r, b_r, run):
    tm = x.shape[0]
    h = x * lax.rsqrt(jnp.mean(x * x, axis=-1, keepdims=True) + EPS) * g2
    logits = jnp.dot(h.astype(BF16), w_r, preferred_element_type=F32) + b_r
    gl, el = logits[:, :LANES], logits[:, LANES:]
    lane = lax.broadcasted_iota(jnp.int32, (tm, LANES), 1).astype(F32)
    ninf = jnp.float32(-jnp.inf)
    big = jnp.float32(LANES)

    def first_argmax(v):
        top = jnp.max(v, axis=-1, keepdims=True)
        return top, jnp.min(jnp.where(v == top, lane, big), axis=-1, keepdims=True)

    gmask = lane < N_GROUPS
    gtop, gsel = first_argmax(jnp.where(gmask, gl, ninf))
    g_w = 1.0 / jnp.sum(jnp.where(gmask, jnp.exp(gl - gtop), 0.0), axis=-1, keepdims=True)
    lo = gsel * EXPERTS_PER_GROUP
    elm = jnp.where((lane >= lo) & (lane < lo + EXPERTS_PER_GROUP), el, ninf)
    t1, i1 = first_argmax(elm)
    t2, i2 = first_argmax(jnp.where(lane == i1, ninf, elm))
    e2 = jnp.exp(t2 - t1)
    gate1 = g_w * (1.0 / (1.0 + e2))
    gate2 = g_w * (e2 / (1.0 + e2))

    hot1, hot2 = lane == i1, lane == i2
    hot = (hot1 | hot2).astype(F32)
    r_i = lax.broadcasted_iota(jnp.int32, (tm, tm), 0)
    c_i = lax.broadcasted_iota(jnp.int32, (tm, tm), 1)
    before = (c_i < r_i).astype(BF16)
    rank = jnp.dot(before, hot.astype(BF16), preferred_element_type=F32) + run
    rank1 = jnp.sum(jnp.where(hot1, rank, 0.0), axis=-1, keepdims=True)
    rank2 = jnp.sum(jnp.where(hot2, rank, 0.0), axis=-1, keepdims=True)
    run = run + jnp.sum(hot, axis=0, keepdims=True)

    meta = jnp.zeros((tm, LANES), F32)
    for k, v in enumerate((i1, i2, rank1, rank2, gate1, gate2)):
        meta = jnp.where(lane == k, v, meta)
    return h, meta, run


def _dispatch_kernel(plan_ref, dest_ref, ha_ref, hb_ref, xs_ref, zero_ref, zsem, sem, *, steps_a):
    i = pl.program_id(0)
    tt = dest_ref.shape[2] // TOP_K
    rows = zero_ref.shape[0]
    p_rows = xs_ref.shape[0]

    def pad_block(e):
        return plan_ref[N_EXPERTS + e] - rows, plan_ref[e] > 0

    def tail_block(c):
        row0 = plan_ref[2 * N_EXPERTS - 1] + c * rows
        return row0, row0 < p_rows

    def for_zero_blocks(block, action):
        def body(c, carry):
            row0, live = block(c)

            @pl.when(live)
            def _():
                dst = xs_ref.at[pl.ds(pl.multiple_of(row0, rows), rows)]
                action(pltpu.make_async_copy(zero_ref, dst, zsem))
            return carry

        lax.fori_loop(0, N_EXPERTS, body, 0)

    @pl.when(i == 0)
    def _():
        zero_ref[...] = jnp.zeros_like(zero_ref)
        for block in (pad_block, tail_block):
            for_zero_blocks(block, lambda cp: cp.start())
        for block in (pad_block, tail_block):
            for_zero_blocks(block, lambda cp: cp.wait())

    def scatter_rows(h_ref):
        def row_copy(tok, k):
            dst = dest_ref[0, 0, TOP_K * tok + k]
            return pltpu.make_async_copy(h_ref.at[pl.ds(tok, 1)], xs_ref.at[pl.ds(dst, 1)], sem)

        def start_rows(tok, c):
            for k in range(TOP_K):
                row_copy(tok, k).start(priority=k)
            return c

        def wait_rows(tok, c):
            for k in range(TOP_K):
                row_copy(tok, k).wait()
            return c

        lax.fori_loop(0, tt, start_rows, 0, unroll=ROW_UNROLL)
        lax.fori_loop(0, tt, wait_rows, 0, unroll=ROW_UNROLL)

    @pl.when(i < steps_a)
    def _():
        scatter_rows(ha_ref)

    @pl.when(i >= steps_a)
    def _():
        scatter_rows(hb_ref)


def _dispatch(plan, dest, h_a, h_b, p_rows):
    tt = TT_ROWS
    steps_a, steps_b = h_a.shape[0] // tt, h_b.shape[0] // tt
    return pl.pallas_call(
        functools.partial(_dispatch_kernel, steps_a=steps_a),
        grid_spec=pltpu.PrefetchScalarGridSpec(
            num_scalar_prefetch=1,
            grid=(steps_a + steps_b,),
            in_specs=[pl.BlockSpec((1, 1, TOP_K * tt), lambda i, plan: (i, 0, 0), memory_space=pltpu.SMEM),
                      pl.BlockSpec((tt, D_PACK), lambda i, plan: (jnp.minimum(i, steps_a - 1), 0)),
                      pl.BlockSpec((tt, D_PACK), lambda i, plan: (jnp.maximum(i - steps_a, 0), 0))],
            out_specs=pl.BlockSpec(memory_space=pl.ANY),
            scratch_shapes=[pltpu.VMEM((MOE_ROWS, D_PACK), jnp.uint32), pltpu.SemaphoreType.DMA(()),
                            pltpu.SemaphoreType.DMA(())]),
        out_shape=jax.ShapeDtypeStruct((p_rows, D_PACK), jnp.uint32),
        compiler_params=_cparams(1),
        name="dispatch",
    )(plan, dest, h_a, h_b)


def _experts_kernel(sched_ref, nused_ref, xs_ref, wg_hbm, wu_hbm, wd_hbm, ys_ref,
                    wg_st, wu_st, wd_st, wg_bf, wu_bf, wd_bf, sem):
    i = pl.program_id(0)
    live = i < nused_ref[0]
    nblk = pl.num_programs(0)
    expert, first, nxt = (sched_ref[r * nblk + i] for r in range(3))
    pairs = ((wg_hbm, wg_st, wg_bf), (wu_hbm, wu_st, wu_bf), (wd_hbm, wd_st, wd_bf))

    def stage_copies(e):
        return [pltpu.make_async_copy(src.at[e], st, sem) for src, st, _ in pairs]

    @pl.when(i == 0)
    def _():
        for cp in stage_copies(expert):
            cp.start()

    @pl.when(live & (first == 1))
    def _():
        for cp in stage_copies(expert):
            cp.wait()
        for _, st, bf in pairs:
            chunk = CAST_ROWS * D_EXPERT // st.shape[1]

            def cast(c, carry, st=st, bf=bf, chunk=chunk):
                rows = pl.ds(pl.multiple_of(c * chunk, chunk), chunk)
                bf[rows, :] = st[rows, :].astype(BF16)
                return carry

            lax.fori_loop(0, st.shape[0] // chunk, cast, 0)

        @pl.when(nxt >= 0)
        def _():
            for cp in stage_copies(nxt):
                cp.start()

    @pl.when(live)
    def _():
        lo, hi = (v.astype(BF16) for v in _unpack_halves(xs_ref[...]))
        up = lambda w: (jnp.dot(lo, w[:D_PACK, :], preferred_element_type=F32)
                        + jnp.dot(hi, w[D_PACK:, :], preferred_element_type=F32))
        g, u = up(wg_bf), up(wu_bf)
        hid = g * (1.0 / (1.0 + jnp.exp(-g))) * u
        ys_ref[...] = _pack_halves(jnp.dot(hid.astype(BF16), wd_bf[...], preferred_element_type=F32))

    @pl.when(jnp.logical_not(live))
    def _():
        ys_ref[...] = jnp.zeros_like(ys_ref)


def _experts(sched, nused, xs, w_gate, w_up, w_down):
    p_rows = xs.shape[0]
    rows = MOE_ROWS
    live = lambda i, sched, nused: jnp.minimum(i, jnp.maximum(nused[0] - 1, 0))
    hbm = pl.BlockSpec(memory_space=pl.ANY)
    return pl.pallas_call(
        _experts_kernel,
        grid_spec=pltpu.PrefetchScalarGridSpec(
            num_scalar_prefetch=2,
            grid=(p_rows // rows,),
            in_specs=[pl.BlockSpec((rows, D_PACK), lambda i, sched, nused: (live(i, sched, nused), 0)), hbm, hbm, hbm],
            out_specs=pl.BlockSpec((rows, D_PACK), lambda i, sched, nused: (i, 0)),
            scratch_shapes=[pltpu.VMEM((D_MODEL, D_EXPERT), F32), pltpu.VMEM((D_MODEL, D_EXPERT), F32),
                            pltpu.VMEM((D_EXPERT, D_MODEL), F32),
                            pltpu.VMEM((D_MODEL, D_EXPERT), BF16), pltpu.VMEM((D_MODEL, D_EXPERT), BF16),
                            pltpu.VMEM((D_EXPERT, D_MODEL), BF16), pltpu.SemaphoreType.DMA(())]),
        out_shape=jax.ShapeDtypeStruct((p_rows, D_PACK), jnp.uint32),
        compiler_params=_cparams(1),
        name="experts",
    )(sched, nused, xs, w_gate, w_up, w_down)


def _combine_kernel(dest_ref, next_ref, x_ref, meta_ref, ys_ref, o_ref, buf_ref, sems):
    i = pl.program_id(0)
    tt = x_ref.shape[0]
    slot = i % 2

    def for_rows(ids_ref, sl, action):
        def body(tok, c):
            for k in range(TOP_K):
                src = ids_ref[0, 0, TOP_K * tok + k]
                action(pltpu.make_async_copy(ys_ref.at[pl.ds(src, 1)], buf_ref.at[sl, k, pl.ds(tok, 1)], sems.at[sl]), k)
            return c

        lax.fori_loop(0, tt, body, 0, unroll=ROW_UNROLL)

    @pl.when(i == 0)
    def _():
        for_rows(dest_ref, slot, lambda cp, k: cp.start(priority=k))

    @pl.when(i + 1 < pl.num_programs(0))
    def _():
        for_rows(next_ref, 1 - slot, lambda cp, k: cp.start(priority=k))

    for_rows(dest_ref, slot, lambda cp, k: cp.wait())
    meta = meta_ref[...]
    g1, g2 = meta[:, 4:5], meta[:, 5:6]
    (lo1, hi1), (lo2, hi2) = _unpack_halves(buf_ref[slot, 0]), _unpack_halves(buf_ref[slot, 1])
    o_ref[:, :D_PACK] = x_ref[:, :D_PACK] + (g1 * lo1 + g2 * lo2)
    o_ref[:, D_PACK:] = x_ref[:, D_PACK:] + (g1 * hi1 + g2 * hi2)


def _combine(dest, x2, meta, ys):
    t = x2.shape[0]
    tt = TT_ROWS
    n = t // tt
    return pl.pallas_call(
        _combine_kernel,
        grid=(n,),
        in_specs=[pl.BlockSpec((1, 1, TOP_K * tt), lambda i: (i, 0, 0), memory_space=pltpu.SMEM),
                  pl.BlockSpec((1, 1, TOP_K * tt), lambda i: (jnp.minimum(i + 1, n - 1), 0, 0),
                               memory_space=pltpu.SMEM),
                  pl.BlockSpec((tt, D_MODEL), lambda i: (i, 0)),
                  pl.BlockSpec((tt, LANES), lambda i: (i, 0)),
                  pl.BlockSpec(memory_space=pl.ANY)],
        out_specs=pl.BlockSpec((tt, D_MODEL), lambda i: (i, 0)),
        out_shape=jax.ShapeDtypeStruct((t, D_MODEL), F32),
        scratch_shapes=[pltpu.VMEM((2, TOP_K, tt, D_PACK), jnp.uint32), pltpu.SemaphoreType.DMA((2,))],
        compiler_params=_cparams(1),
        name="combine",
    )(dest, dest, x2, meta, ys)


def _rope_tables(s):
    rows = s // GRID_W
    r, c = jnp.meshgrid(jnp.arange(rows, dtype=F32), jnp.arange(GRID_W, dtype=F32), indexing='ij')
    n_freq = HEAD_DIM // 4
    inv = ROPE_THETA ** (-jnp.arange(n_freq, dtype=F32) / n_freq)
    ang_r = r.reshape(-1)[:, None] * inv
    ang_c = c.reshape(-1)[:, None] * inv
    ang = jnp.concatenate([ang_r, ang_r, ang_c, ang_c], axis=-1)
    sign = np.tile(np.repeat(np.array([-1.0, 1.0], np.float32), n_freq), 2)
    return jnp.cos(ang), jnp.sin(ang) * sign


def _moe_plan(metas, cnt):
    rows = MOE_ROWS
    t = sum(m.shape[0] for m in metas)
    p_rows = TOP_K * t + N_EXPERTS * rows
    nblk = p_rows // rows
    counts = cnt[0, :N_EXPERTS].astype(jnp.int32)
    pcounts = (counts + rows - 1) // rows * rows
    pends = jnp.cumsum(pcounts)
    pstarts = pends - pcounts
    ids = jnp.arange(N_EXPERTS, dtype=jnp.int32)

    def slots(meta):
        expert = meta[:, 0:TOP_K].astype(jnp.int32)
        rank = meta[:, TOP_K:2 * TOP_K].astype(jnp.int32)
        start = jnp.sum(jnp.where(expert[..., None] == ids, pstarts, 0), axis=-1)
        return (start + rank).reshape(meta.shape[0] // TT_ROWS, 1, TOP_K * TT_ROWS)

    dest = [slots(m) for m in metas]
    nused = pends[-1] // rows
    row0 = jnp.arange(nblk, dtype=jnp.int32) * rows
    blk = jnp.sum(pends[None, :] <= jnp.minimum(row0, pends[-1] - rows)[:, None], axis=1).astype(jnp.int32)
    plan = jnp.concatenate([pcounts, pends]).astype(jnp.int32)
    blocks = jnp.arange(nblk, dtype=jnp.int32)
    first = ((blocks == 0) | (blk != jnp.roll(blk, 1))) & (blocks < nused)
    later = (pcounts[None, :] > 0) & (ids[None, :] > ids[:, None])
    next_used = jnp.min(jnp.where(later, ids[None, :], N_EXPERTS), axis=1)
    next_used = jnp.where(next_used == N_EXPERTS, -1, next_used)
    nxt = jnp.sum(jnp.where(blk[:, None] == ids[None, :], next_used[None, :], 0), axis=1)
    sched = jnp.concatenate([blk, first.astype(jnp.int32), nxt]).astype(jnp.int32)
    return plan, dest, sched, nused.reshape(1).astype(jnp.int32), p_rows


def _attend_and_route(x, p, run0):
    b, s, _ = x.shape
    t = b * s
    assert s % (TQ_DIL * max(d for _, d in DILATED)) == 0 and s % TM_IN == 0 and t % TT_ROWS == 0
    x = x.reshape(t, D_MODEL)
    cos, sin = _rope_tables(s)
    qkv, r4, r16 = _in_proj(x, p['norm1_g'], p['w_in'], p['gain'], cos, sin, b, s)
    residue_major = {1: qkv.reshape(N_SEC, b, 1, s, SEC), 4: r4, 16: r16}
    branches = [_dilated_branch(residue_major[d], p['band'][i], b, s, d) for i, (_, d) in enumerate(DILATED)]
    ob = _gqa(qkv, b, s)
    return _out_proj([o for o, _ in branches], [l for _, l in branches], ob, x, p['on_a'], p['on_b'],
                     p['w_out'], p['norm2_g'], p['w_r'], p['b_r'], run0, s)


def _layer(x_a, x_b, p):
    x2_a, h_a, meta_a, cnt_a = _attend_and_route(x_a, p, jnp.zeros((1, LANES), F32))
    x2_b, h_b, meta_b, cnt = _attend_and_route(x_b, p, cnt_a)
    plan, (dest_a, dest_b), sched, nused, p_rows = _moe_plan([meta_a, meta_b], cnt)
    xs = _dispatch(plan, jnp.concatenate([dest_a, dest_b]), h_a, h_b, p_rows)
    ys = _experts(sched, nused, xs, p['w_gate'], p['w_up'], p['w_down'])
    return (_combine(dest_a, x2_a, meta_a, ys).reshape(x_a.shape),
            _combine(dest_b, x2_b, meta_b, ys).reshape(x_b.shape))


def kernel(x_prompt, x_sample, norm1_g, w_in, qn_a, kn_a, qn_b, kn_b, rel_bias, on_a, on_b, w_out, norm2_g, rg_w,
           rg_b, re_w, re_b, w_gate, w_up, w_down):
    assert norm1_g.shape[0] == 1, "one layer"
    scale = HEAD_DIM ** -0.5
    ones_a = jnp.ones((WIDTH_A,), F32)
    gain = jnp.concatenate([jnp.tile(qn_a[0], N_HEADS_A) * scale, jnp.tile(kn_a[0], N_HEADS_A), ones_a,
                            jnp.tile(qn_b[0], N_HEADS_B) * (scale * LOG2E), jnp.tile(kn_b[0], N_KV_B),
                            jnp.ones((KV_WIDTH_B,), F32)]).reshape(1, IN_WIDTH)
    pad_g = jnp.zeros((D_MODEL, LANES - N_GROUPS), F32)
    pad_e = jnp.zeros((D_MODEL, LANES - N_EXPERTS), F32)
    params = dict(
        norm1_g=norm1_g, w_in=w_in[0].astype(BF16), gain=gain,
        band=[_band_tables(rel_bias, d) for _, d in DILATED],
        on_a=on_a, on_b=on_b, w_out=w_out[0].astype(BF16), norm2_g=norm2_g,
        w_r=jnp.concatenate([rg_w[0], pad_g, re_w[0], pad_e], axis=1).astype(BF16),
        b_r=jnp.concatenate([rg_b[0], jnp.zeros((LANES - N_GROUPS,), F32), re_b[0],
                             jnp.zeros((LANES - N_EXPERTS,), F32)]).reshape(1, 2 * LANES),
        w_gate=w_gate[0], w_up=w_up[0], w_down=w_down[0])
    return _layer(x_prompt, x_sample, params)
```
